```python
import math
import jax, jax.numpy as jnp
from jax import lax
import numpy as np

D_MODEL = 1024
BATCH = 4
SEQ = 8192
DEPTH = 2

HEAD_DIM = 64
ROT_DIM = HEAD_DIM // 4
ROPE_THETA = 500000.0
NORM_EPS = 1e-6
D_FF = 2816

MOBA_HEADS = 8
MOBA_BLOCK = 256
MOBA_TOPK = 3
MOBA_Q_CHUNK = 64
MOBA_WIDTH = MOBA_HEADS * HEAD_DIM
GMLP_GROUPS = 8
GMLP_GROUP_DIM = 64
GMLP_CHUNK = 128
GMLP_WIDTH = GMLP_GROUPS * GMLP_GROUP_DIM
EVEN_IN = 3 * MOBA_WIDTH + 2 * GMLP_WIDTH
EVEN_OUT = MOBA_WIDTH + GMLP_WIDTH

DIFF_HEADS = 8
DIFF_QK_DIM = HEAD_DIM
DIFF_V_DIM = 2 * HEAD_DIM
DIFF_QK_WIDTH = DIFF_HEADS * 2 * DIFF_QK_DIM
DIFF_WIDTH = DIFF_HEADS * DIFF_V_DIM
ODD_IN = 2 * DIFF_QK_WIDTH + DIFF_WIDTH
DIFF_Q_BLOCK = 128

N_EVEN = (DEPTH + 1) // 2
N_ODD = DEPTH // 2

kernel_name = 'hybrid_moba_gmlp_diffattn_macaron'


def rms_norm(x, g):
    xf = x.astype(jnp.float32)
    y = xf * lax.rsqrt(jnp.mean(xf * xf, axis=-1, keepdims=True) + NORM_EPS)
    return (y * g.astype(jnp.float32)).astype(x.dtype)


def layer_norm(x, g, b):
    xf = x.astype(jnp.float32)
    mu = jnp.mean(xf, axis=-1, keepdims=True)
    var = jnp.mean(jnp.square(xf - mu), axis=-1, keepdims=True)
    y = (xf - mu) * lax.rsqrt(var + NORM_EPS)
    return (y * g.astype(jnp.float32) + b.astype(jnp.float32)).astype(x.dtype)


def swiglu(x, w_gate, w_up, w_down):
    return (jax.nn.silu(x @ w_gate) * (x @ w_up)) @ w_down


def rope_tables(seq_len):
    inv = 1.0 / (ROPE_THETA ** (jnp.arange(0, ROT_DIM, 2, dtype=jnp.float32) / ROT_DIM))
    ang = jnp.arange(seq_len, dtype=jnp.float32)[:, None] * inv[None, :]
    return jnp.cos(ang), jnp.sin(ang)


def apply_partial_rope(x, cos, sin):
    half = ROT_DIM // 2
    x1, x2, rest = x[..., :half], x[..., half:ROT_DIM], x[..., ROT_DIM:]
    c, s = cos.astype(x.dtype), sin.astype(x.dtype)
    return jnp.concatenate([x1 * c - x2 * s, x1 * s + x2 * c, rest], axis=-1)


_gather_blocks = jax.vmap(jax.vmap(lambda tab, idx: tab[idx]))


def moba_attention(q, k, v):
    B, H, S, Dh = q.shape
    s_pad = -(-S // MOBA_BLOCK) * MOBA_BLOCK
    pad = ((0, 0), (0, 0), (0, s_pad - S), (0, 0))
    q, k, v = jnp.pad(q, pad), jnp.pad(k, pad), jnp.pad(v, pad)
    n_blocks = s_pad // MOBA_BLOCK
    kb = k.reshape(B, H, n_blocks, MOBA_BLOCK, Dh)
    vb = v.reshape(B, H, n_blocks, MOBA_BLOCK, Dh)
    k_mean = jnp.mean(kb.astype(jnp.float32), axis=3).astype(q.dtype)
    scale = Dh ** -0.5
    n_top = min(MOBA_TOPK, n_blocks - 1)
    block_ids = jnp.arange(n_blocks)

    def chunk(c):
        start = c * MOBA_Q_CHUNK
        blk = start // MOBA_BLOCK
        qc = lax.dynamic_slice_in_dim(q, start, MOBA_Q_CHUNK, axis=2)
        q_pos = start + jnp.arange(MOBA_Q_CHUNK)
        k_pos = blk * MOBA_BLOCK + jnp.arange(MOBA_BLOCK)
        k_own = lax.dynamic_index_in_dim(kb, blk, axis=2, keepdims=False)
        v_own = lax.dynamic_index_in_dim(vb, blk, axis=2, keepdims=False)
        s_own = jnp.einsum('bhqd,bhkd->bhqk', qc, k_own).astype(jnp.float32) * scale
        s_own = jnp.where(k_pos[None, :] <= q_pos[:, None], s_own, -jnp.inf)
        if n_top == 0:
            p_own = jax.nn.softmax(s_own, axis=-1).astype(v.dtype)
            return jnp.einsum('bhqk,bhkd->bhqd', p_own, v_own)
        gate = jnp.einsum('bhqd,bhnd->bhqn', qc, k_mean).astype(jnp.float32)
        gate = jnp.where(block_ids < blk, gate, -jnp.inf)
        _, idx = lax.top_k(gate, n_top)
        valid = idx < blk
        k_sel = _gather_blocks(kb, idx)
        v_sel = _gather_blocks(vb, idx)
        s_sel = jnp.einsum('bhqd,bhqnkd->bhqnk', qc, k_sel).astype(jnp.float32) * scale
        s_sel = jnp.where(valid[..., None], s_sel, -jnp.inf)
        scores = jnp.concatenate(
            [s_sel.reshape(B, H, MOBA_Q_CHUNK, n_top * MOBA_BLOCK), s_own], axis=-1)
        p = jax.nn.softmax(scores, axis=-1).astype(v.dtype)
        p_sel = p[..., :n_top * MOBA_BLOCK].reshape(B, H, MOBA_Q_CHUNK, n_top, MOBA_BLOCK)
        p_own = p[..., n_top * MOBA_BLOCK:]
        return (jnp.einsum('bhqnk,bhqnkd->bhqd', p_sel, v_sel)
                + jnp.einsum('bhqk,bhkd->bhqd', p_own, v_own))

    n_chunks = s_pad // MOBA_Q_CHUNK
    out = lax.map(chunk, jnp.arange(n_chunks))
    out = out.transpose(1, 2, 0, 3, 4).reshape(B, H, s_pad, Dh)
    return out[:, :, :S]


def chunked_spatial_gating(z, ln_g, ln_b, w_s, b_s):
    B, S, _ = z.shape
    u, v = z[..., :GMLP_WIDTH], z[..., GMLP_WIDTH:]
    v = layer_norm(v.reshape(B, S, GMLP_GROUPS, GMLP_GROUP_DIM), ln_g, ln_b)
    v = v.reshape(B, S // GMLP_CHUNK, GMLP_CHUNK, GMLP_GROUPS, GMLP_GROUP_DIM)
    causal = jnp.tril(jnp.ones((GMLP_CHUNK, GMLP_CHUNK), dtype=bool))
    w = jnp.where(causal[None], w_s, jnp.zeros_like(w_s))
    mixed = jnp.einsum('gts,bnsgd->bntgd', w, v) + b_s.T[None, None, :, :, None]
    return u * mixed.reshape(B, S, GMLP_WIDTH)


def diff_attention(q, k, v, lam, subln_g, lambda_init):
    B, H, _, S, Dqk = q.shape
    scale = Dqk ** -0.5
    k_pos = jnp.arange(S)

    def block(i):
        start = i * DIFF_Q_BLOCK
        qb = lax.dynamic_slice_in_dim(q, start, DIFF_Q_BLOCK, axis=3)
        s = jnp.einsum('bhcqd,bhckd->bhcqk', qb, k).astype(jnp.float32) * scale
        q_pos = start + jnp.arange(DIFF_Q_BLOCK)
        s = jnp.where(k_pos[None, :] <= q_pos[:, None], s, -jnp.inf)
        p = jax.nn.softmax(s, axis=-1)
        a = (p[:, :, 0] - lam * p[:, :, 1]).astype(v.dtype)
        o = jnp.einsum('bhqk,bhkd->bhqd', a, v)
        return rms_norm(o, subln_g) * (1.0 - lambda_init)

    out = lax.map(block, jnp.arange(S // DIFF_Q_BLOCK))
    return out.transpose(1, 0, 3, 2, 4).reshape(B, S, H * v.shape[-1])


def even_mixer(h, cos, sin, w_in, w_out, ln_g, ln_b, w_s, b_s):
    B, S, _ = h.shape
    z = h @ w_in
    q = z[..., :MOBA_WIDTH]
    k = z[..., MOBA_WIDTH:2 * MOBA_WIDTH]
    v = z[..., 2 * MOBA_WIDTH:3 * MOBA_WIDTH]
    gz = z[..., 3 * MOBA_WIDTH:]
    to_heads = lambda t: t.reshape(B, S, MOBA_HEADS, HEAD_DIM).transpose(0, 2, 1, 3)
    q = apply_partial_rope(to_heads(q), cos, sin)
    k = apply_partial_rope(to_heads(k), cos, sin)
    attn = moba_attention(q, k, to_heads(v))
    attn = attn.transpose(0, 2, 1, 3).reshape(B, S, MOBA_WIDTH)
    gated = chunked_spatial_gating(jax.nn.gelu(gz), ln_g, ln_b, w_s, b_s)
    return jnp.concatenate([attn, gated], axis=-1) @ w_out


def odd_mixer(h, cos, sin, w_in, w_out, lq1, lk1, lq2, lk2, subln_g, lambda_init):
    B, S, _ = h.shape
    z = h @ w_in
    to_qk = lambda t: t.reshape(B, S, DIFF_HEADS, 2, DIFF_QK_DIM).transpose(0, 2, 3, 1, 4)
    q = apply_partial_rope(to_qk(z[..., :DIFF_QK_WIDTH]), cos, sin)
    k = apply_partial_rope(to_qk(z[..., DIFF_QK_WIDTH:2 * DIFF_QK_WIDTH]), cos, sin)
    v = z[..., 2 * DIFF_QK_WIDTH:].reshape(B, S, DIFF_HEADS, DIFF_V_DIM).transpose(0, 2, 1, 3)
    f32 = jnp.float32
    lam = (jnp.exp(jnp.sum(lq1.astype(f32) * lk1.astype(f32)))
           - jnp.exp(jnp.sum(lq2.astype(f32) * lk2.astype(f32))) + lambda_init)
    return diff_attention(q, k, v, lam, subln_g, lambda_init) @ w_out


def setup_inputs(seed: int = 0) -> dict:
    key = jax.random.key(seed)
    ks = iter(jax.random.split(key, 32))
    nrm = lambda shape, scale: jax.random.normal(next(ks), shape, jnp.float32) * scale
    gain = lambda shape: 1.0 + nrm(shape, 0.02)
    return {
        'x': nrm((BATCH, SEQ, D_MODEL), 1.0),
        'ffn_pre_norm': gain((DEPTH, D_MODEL)),
        'ffn_pre_w_gate': nrm((DEPTH, D_MODEL, D_FF), D_MODEL ** -0.5),
        'ffn_pre_w_up': nrm((DEPTH, D_MODEL, D_FF), D_MODEL ** -0.5),
        'ffn_pre_w_down': nrm((DEPTH, D_FF, D_MODEL), D_FF ** -0.5),
        'mix_norm': gain((DEPTH, D_MODEL)),
        'ffn_post_norm': gain((DEPTH, D_MODEL)),
        'ffn_post_w_gate': nrm((DEPTH, D_MODEL, D_FF), D_MODEL ** -0.5),
        'ffn_post_w_up': nrm((DEPTH, D_MODEL, D_FF), D_MODEL ** -0.5),
        'ffn_post_w_down': nrm((DEPTH, D_FF, D_MODEL), D_FF ** -0.5),
        'even_w_in': nrm((N_EVEN, D_MODEL, EVEN_IN), D_MODEL ** -0.5),
        'even_w_out': nrm((N_EVEN, EVEN_OUT, D_MODEL), EVEN_OUT ** -0.5),
        'gmlp_ln_g': gain((N_EVEN, GMLP_GROUPS, GMLP_GROUP_DIM)),
        'gmlp_ln_b': nrm((N_EVEN, GMLP_GROUPS, GMLP_GROUP_DIM), 0.02),
        'gmlp_w_s': nrm((N_EVEN, GMLP_GROUPS, GMLP_CHUNK, GMLP_CHUNK), GMLP_CHUNK ** -0.5),
        'gmlp_b_s': 1.0 + nrm((N_EVEN, GMLP_GROUPS, GMLP_CHUNK), 0.02),
        'odd_w_in': nrm((N_ODD, D_MODEL, ODD_IN), D_MODEL ** -0.5),
        'odd_w_out': nrm((N_ODD, DIFF_WIDTH, D_MODEL), DIFF_WIDTH ** -0.5),
        'diff_lambda_q1': nrm((N_ODD, DIFF_QK_DIM), 0.1),
        'diff_lambda_k1': nrm((N_ODD, DIFF_QK_DIM), 0.1),
        'diff_lambda_q2': nrm((N_ODD, DIFF_QK_DIM), 0.1),
        'diff_lambda_k2': nrm((N_ODD, DIFF_QK_DIM), 0.1),
        'diff_subln_g': gain((N_ODD, DIFF_V_DIM)),
        'final_norm': gain((D_MODEL,)),
    }


def reference(x, ffn_pre_norm, ffn_pre_w_gate, ffn_pre_w_up, ffn_pre_w_down, mix_norm,
              ffn_post_norm, ffn_post_w_gate, ffn_post_w_up, ffn_post_w_down,
              even_w_in, even_w_out, gmlp_ln_g, gmlp_ln_b, gmlp_w_s, gmlp_b_s,
              odd_w_in, odd_w_out, diff_lambda_q1, diff_lambda_k1, diff_lambda_q2,
              diff_lambda_k2, diff_subln_g, final_norm):
    cos, sin = rope_tables(x.shape[1])
    for layer in range(DEPTH):
        x = x + 0.5 * swiglu(rms_norm(x, ffn_pre_norm[layer]), ffn_pre_w_gate[layer],
                             ffn_pre_w_up[layer], ffn_pre_w_down[layer])
        h = rms_norm(x, mix_norm[layer])
        if layer % 2 == 0:
            e = layer // 2
            x = x + even_mixer(h, cos, sin, even_w_in[e], even_w_out[e], gmlp_ln_g[e],
                               gmlp_ln_b[e], gmlp_w_s[e], gmlp_b_s[e])
        else:
            o = layer // 2
            lambda_init = 0.8 - 0.6 * math.exp(-0.3 * layer)
            x = x + odd_mixer(h, cos, sin, odd_w_in[o], odd_w_out[o], diff_lambda_q1[o],
                              diff_lambda_k1[o], diff_lambda_q2[o], diff_lambda_k2[o],
                              diff_subln_g[o], lambda_init)
        x = x + 0.5 * swiglu(rms_norm(x, ffn_post_norm[layer]), ffn_post_w_gate[layer],
                             ffn_post_w_up[layer], ffn_post_w_down[layer])
    return rms_norm(x, final_norm)
```

```python
import functools
import math

import jax
import jax.numpy as jnp
from jax import lax
from jax.experimental import pallas as pl
from jax.experimental.pallas import tpu as pltpu

F32 = jnp.float32
BF16 = jnp.bfloat16

HEAD_DIM = 64
ROT_DIM = HEAD_DIM // 4
ROPE_THETA = 500000.0
NORM_EPS = 1e-6
MOBA_BLOCK = 256
MOBA_TOPK = 3
GMLP_GROUPS = 8
GMLP_GROUP_DIM = 64
GMLP_CHUNK = 128

LANES = 128
VMEM_LIMIT_BYTES = 56 * 1024 * 1024

NEG_INF = float("-inf")


def _params(n_grid_axes):
    return pltpu.CompilerParams(
        dimension_semantics=("parallel",) * (n_grid_axes - 1) + ("arbitrary",),
        vmem_limit_bytes=VMEM_LIMIT_BYTES,
    )


def _resident():
    return pl.BlockSpec(memory_space=pltpu.VMEM)


def _rms_norm(x, g):
    ms = jnp.mean(x * x, axis=-1, keepdims=True)
    return x * lax.rsqrt(ms + NORM_EPS) * g


def _dot(a, b):
    return jnp.dot(a, b, preferred_element_type=F32)


def _dot_nt(a, b):
    return lax.dot_general(a, b, (((1,), (1,)), ((), ())), preferred_element_type=F32)


def _ffn_kernel(x_ref, g_ref, wg_ref, wu_ref, wd_ref, fin_ref, o_ref, *, final_norm):
    x = x_ref[...]
    h = _rms_norm(x, g_ref[...]).astype(BF16)
    gate = _dot(h, wg_ref[...])
    up = _dot(h, wu_ref[...])
    act = (gate * jax.nn.sigmoid(gate) * up).astype(BF16)
    y = x + 0.5 * _dot(act, wd_ref[...])
    if final_norm:
        y = _rms_norm(y, fin_ref[...])
    o_ref[...] = y


def _ffn(x, g, wg, wu, wd, fin, *, final_norm, tm):
    t, d = x.shape
    row = pl.BlockSpec((tm, d), lambda i: (i, 0))
    return pl.pallas_call(
        functools.partial(_ffn_kernel, final_norm=final_norm),
        grid=(t // tm,),
        in_specs=[row, _resident(), _resident(), _resident(), _resident(), _resident()],
        out_specs=row,
        out_shape=jax.ShapeDtypeStruct((t, d), F32),
        compiler_params=_params(1),
        name="ffn_final" if final_norm else "ffn",
    )(x, g, wg, wu, wd, fin)


def _rope_tables(seq_len):
    half = ROT_DIM // 2
    inv = 1.0 / (ROPE_THETA ** (jnp.arange(0, ROT_DIM, 2, dtype=F32) / ROT_DIM))
    ang = jnp.arange(seq_len, dtype=F32)[:, None] * inv[None, :]
    cos, sin = jnp.cos(ang), jnp.sin(ang)
    ones = jnp.ones((seq_len, HEAD_DIM - ROT_DIM), F32)
    zeros_h = jnp.zeros((seq_len, half), F32)
    zeros_r = jnp.zeros((seq_len, HEAD_DIM - ROT_DIM), F32)
    c_same = jnp.concatenate([cos, cos, ones], axis=-1)
    c_prev = jnp.concatenate([zeros_h, sin, zeros_r], axis=-1)
    c_next = jnp.concatenate([-sin, zeros_h, zeros_r], axis=-1)
    rep = LANES // HEAD_DIM
    return tuple(jnp.tile(c, (1, rep)) for c in (c_same, c_prev, c_next))


def _rope(zc, c_same, c_prev, c_next):
    half = ROT_DIM // 2
    return (zc * c_same + pltpu.roll(zc, half, 1) * c_prev
            + pltpu.roll(zc, LANES - half, 1) * c_next)


def _split_bf16(a):
    hi = a.astype(BF16)
    lo = (a - hi.astype(F32)).astype(BF16)
    return hi, lo


def _group_mean(a, avg):
    hi, lo = _split_bf16(a)
    return _dot(hi, avg) + _dot(lo, avg)


def _even_in_kernel(x_ref, g_ref, w_ref, cs_ref, cp_ref, cn_ref, avg_ref, lng_ref, lnb_ref, ws_ref,
                    bias_ref, q_ref, k_ref, v_ref, km_ref, gated_ref, *, width, q_scale):
    tm = x_ref.shape[0]
    h = _rms_norm(x_ref[...], g_ref[...]).astype(BF16)
    z = _dot(h, w_ref[...])
    c_same, c_prev, c_next = cs_ref[...], cp_ref[...], cn_ref[...]
    n_lane_blocks = width // LANES
    for cb in range(n_lane_blocks):
        lo = cb * LANES
        q_ref[:, lo:lo + LANES] = (_rope(z[:, lo:lo + LANES], c_same, c_prev, c_next) * q_scale).astype(BF16)
        kr = _rope(z[:, width + lo:width + lo + LANES], c_same, c_prev, c_next)
        k_ref[:, lo:lo + LANES] = kr.astype(BF16)
        for blk in range(tm // MOBA_BLOCK):
            rows = kr[blk * MOBA_BLOCK:(blk + 1) * MOBA_BLOCK]
            km_ref[blk, :, lo:lo + LANES] = jnp.mean(rows, axis=0, keepdims=True)
    v_ref[...] = z[:, 2 * width:3 * width].astype(BF16)

    gz = z[:, 3 * width:]
    gz = 0.5 * gz * (1.0 + jnp.tanh(math.sqrt(2.0 / math.pi) * (gz + 0.044715 * (gz * gz * gz))))
    u, vv = gz[:, :width], gz[:, width:]
    avg = avg_ref[...]
    cen = vv - _group_mean(vv, avg)
    var = _group_mean(cen * cen, avg)
    vn = (cen * lax.rsqrt(var + NORM_EPS) * lng_ref[...] + lnb_ref[...]).astype(BF16)

    t_idx = lax.broadcasted_iota(jnp.int32, (GMLP_CHUNK, GMLP_CHUNK), 0)
    s_idx = lax.broadcasted_iota(jnp.int32, (GMLP_CHUNK, GMLP_CHUNK), 1)
    causal = s_idx <= t_idx
    first_group = lax.broadcasted_iota(jnp.int32, (GMLP_CHUNK, LANES), 1) < GMLP_GROUP_DIM
    for pair in range(n_lane_blocks):
        lo = pair * LANES
        w0 = jnp.where(causal, ws_ref[2 * pair], 0.0).astype(BF16)
        w1 = jnp.where(causal, ws_ref[2 * pair + 1], 0.0).astype(BF16)
        bias = bias_ref[:, lo:lo + LANES]
        for c in range(tm // GMLP_CHUNK):
            r0 = c * GMLP_CHUNK
            vc = vn[r0:r0 + GMLP_CHUNK, lo:lo + LANES]
            mixed = jnp.where(first_group, _dot(w0, vc), _dot(w1, vc)) + bias
            gated_ref[r0:r0 + GMLP_CHUNK, lo:lo + LANES] = (u[r0:r0 + GMLP_CHUNK, lo:lo + LANES] * mixed).astype(BF16)


def _even_in(x, g, w_in, tables, avg, ln_g, ln_b, w_s, bias, *, seq_len, tm):
    t, d = x.shape
    width = w_in.shape[1] // 5
    n_seq_tiles = seq_len // tm
    row = pl.BlockSpec((tm, d), lambda i: (i, 0))
    tab = pl.BlockSpec((tm, LANES), lambda i: (i % n_seq_tiles, 0))
    out_row = pl.BlockSpec((tm, width), lambda i: (i, 0))
    nb = tm // MOBA_BLOCK
    act = jax.ShapeDtypeStruct((t, width), BF16)
    return pl.pallas_call(
        functools.partial(_even_in_kernel, width=width, q_scale=HEAD_DIM ** -0.5),
        grid=(t // tm,),
        in_specs=[row, _resident(), _resident(), tab, tab, tab, _resident(), _resident(), _resident(),
                  _resident(), _resident()],
        out_specs=[out_row, out_row, out_row, pl.BlockSpec((nb, 1, width), lambda i: (i, 0, 0)), out_row],
        out_shape=[act, act, act, jax.ShapeDtypeStruct((t // MOBA_BLOCK, 1, width), F32), act],
        compiler_params=_params(1),
        name="even_in",
    )(x, g, w_in, *tables, avg, ln_g, ln_b, w_s, bias)


def _select_blocks(gate, n_past):
    block_id = lax.broadcasted_iota(jnp.int32, gate.shape, 1)
    n_blocks = gate.shape[1]
    g = jnp.where(block_id < n_past, gate, NEG_INF)
    mask = jnp.full(gate.shape, NEG_INF, F32)
    for _ in range(MOBA_TOPK):
        best = jnp.max(g, axis=-1, keepdims=True)
        cand = jnp.where((g == best) & (best > NEG_INF), block_id, n_blocks)
        pick = block_id == jnp.min(cand, axis=-1, keepdims=True)
        mask = jnp.where(pick, 0.0, mask)
        g = jnp.where(pick, NEG_INF, g)
    return mask


def _moba_kernel(q_ref, k_ref, v_ref, km_ref, o_ref):
    tq = q_ref.shape[1]
    qi = pl.program_id(2)
    q = q_ref[0]
    lane = lax.broadcasted_iota(jnp.int32, (tq, LANES), 1)
    head_lanes = (lane < HEAD_DIM, lane >= HEAD_DIM)
    km = km_ref[0].astype(BF16)
    qh = [jnp.where(m, q, jnp.zeros_like(q)) for m in head_lanes]
    sel = [_select_blocks(_dot_nt(qh[h], km), qi) for h in range(2)]
    block_id = lax.broadcasted_iota(jnp.int32, sel[0].shape, 1)

    own = pl.multiple_of(qi * tq, tq)
    k_own = k_ref[0, pl.ds(own, tq), :]
    v_own = v_ref[0, pl.ds(own, tq), :]
    row = lax.broadcasted_iota(jnp.int32, (tq, tq), 0)
    col = lax.broadcasted_iota(jnp.int32, (tq, tq), 1)
    carry = []
    for h in range(2):
        s = jnp.where(col <= row, _dot_nt(qh[h], k_own), NEG_INF)
        m = jnp.max(s, axis=-1, keepdims=True)
        p = jnp.exp(s - m)
        carry += [m, jnp.sum(p, axis=-1, keepdims=True), _dot(p.astype(BF16), v_own)]

    def body(j, carry):
        start = pl.multiple_of(j * tq, tq)
        kj = k_ref[0, pl.ds(start, tq), :]
        vj = v_ref[0, pl.ds(start, tq), :]
        out = []
        for h in range(2):
            m, l, acc = carry[3 * h:3 * h + 3]
            row_mask = jnp.max(jnp.where(block_id == j, sel[h], NEG_INF), axis=-1, keepdims=True)
            s = _dot_nt(qh[h], kj) + row_mask
            m_new = jnp.maximum(m, jnp.max(s, axis=-1, keepdims=True))
            alpha = jnp.exp(m - m_new)
            p = jnp.exp(s - m_new)
            out += [m_new, alpha * l + jnp.sum(p, axis=-1, keepdims=True),
                    alpha * acc + _dot(p.astype(BF16), vj)]
        return tuple(out)

    carry = lax.fori_loop(0, qi, body, tuple(carry))
    o0 = carry[2] / carry[1]
    o1 = carry[5] / carry[4]
    o_ref[0] = jnp.where(head_lanes[0], o0, o1).astype(BF16)


def _moba(q, k, v, km):
    b, s, width = q.shape
    n_blocks = s // MOBA_BLOCK
    qspec = pl.BlockSpec((1, MOBA_BLOCK, LANES), lambda bi, hp, qi: (bi, qi, hp))
    kvspec = pl.BlockSpec((1, s, LANES), lambda bi, hp, qi: (bi, 0, hp))
    return pl.pallas_call(
        _moba_kernel,
        grid=(b, width // LANES, n_blocks),
        in_specs=[qspec, kvspec, kvspec, pl.BlockSpec((1, n_blocks, LANES), lambda bi, hp, qi: (bi, 0, hp))],
        out_specs=qspec,
        out_shape=jax.ShapeDtypeStruct((b, s, width), BF16),
        compiler_params=_params(3),
        name="moba_attn",
    )(q, k, v, km)


def _out_proj_kernel(*refs, n_in):
    x_ref, o_ref = refs[0], refs[-1]
    y = x_ref[...]
    for i in range(n_in):
        y = y + _dot(refs[1 + i][...], refs[1 + n_in + i][...])
    o_ref[...] = y


def _out_proj(x, acts, weights, *, tm):
    t, d = x.shape
    row = pl.BlockSpec((tm, d), lambda i: (i, 0))
    act_specs = [pl.BlockSpec((tm, a.shape[1]), lambda i: (i, 0)) for a in acts]
    return pl.pallas_call(
        functools.partial(_out_proj_kernel, n_in=len(acts)),
        grid=(t // tm,),
        in_specs=[row] + act_specs + [_resident()] * len(weights),
        out_specs=row,
        out_shape=jax.ShapeDtypeStruct((t, d), F32),
        compiler_params=_params(1),
        name="out_proj",
    )(x, *acts, *weights)


def _odd_in_kernel(x_ref, g_ref, w_ref, cs_ref, cp_ref, cn_ref, q_ref, k_ref, v_ref, *, qk_width, q_scale):
    h = _rms_norm(x_ref[...], g_ref[...]).astype(BF16)
    z = _dot(h, w_ref[...])
    c_same, c_prev, c_next = cs_ref[...], cp_ref[...], cn_ref[...]
    for cb in range(qk_width // LANES):
        lo = cb * LANES
        q_ref[:, lo:lo + LANES] = (_rope(z[:, lo:lo + LANES], c_same, c_prev, c_next) * q_scale).astype(BF16)
        k_ref[:, lo:lo + LANES] = _rope(z[:, qk_width + lo:qk_width + lo + LANES], c_same, c_prev, c_next).astype(BF16)
    v_ref[...] = z[:, 2 * qk_width:].astype(BF16)


def _odd_in(x, g, w_in, tables, *, qk_width, seq_len, tm):
    t, d = x.shape
    v_width = w_in.shape[1] - 2 * qk_width
    n_seq_tiles = seq_len // tm
    row = pl.BlockSpec((tm, d), lambda i: (i, 0))
    tab = pl.BlockSpec((tm, LANES), lambda i: (i % n_seq_tiles, 0))
    qk_row = pl.BlockSpec((tm, qk_width), lambda i: (i, 0))
    qk = jax.ShapeDtypeStruct((t, qk_width), BF16)
    return pl.pallas_call(
        functools.partial(_odd_in_kernel, qk_width=qk_width, q_scale=HEAD_DIM ** -0.5),
        grid=(t // tm,),
        in_specs=[row, _resident(), _resident(), tab, tab, tab],
        out_specs=[qk_row, qk_row, pl.BlockSpec((tm, v_width), lambda i: (i, 0))],
        out_shape=[qk, qk, jax.ShapeDtypeStruct((t, v_width), BF16)],
        compiler_params=_params(1),
        name="odd_in",
    )(x, g, w_in, *tables)


def _diff_kernel(lq1_ref, lk1_ref, lq2_ref, lk2_ref, g_ref, q_ref, k_ref, v_ref, o_ref, *, lambda_init):
    tq = q_ref.shape[1]
    qi = pl.program_id(2)
    q = q_ref[0]
    lane = lax.broadcasted_iota(jnp.int32, (tq, LANES), 1)
    qm = [jnp.where(lane < HEAD_DIM, q, jnp.zeros_like(q)), jnp.where(lane >= HEAD_DIM, q, jnp.zeros_like(q))]

    own = pl.multiple_of(qi * tq, tq)
    k_own = k_ref[0, pl.ds(own, tq), :]
    v_own = v_ref[0, pl.ds(own, tq), :]
    row = lax.broadcasted_iota(jnp.int32, (tq, tq), 0)
    col = lax.broadcasted_iota(jnp.int32, (tq, tq), 1)
    carry = []
    for c in range(2):
        s = jnp.where(col <= row, _dot_nt(qm[c], k_own), NEG_INF)
        m = jnp.max(s, axis=-1, keepdims=True)
        p = jnp.exp(s - m)
        carry += [m, jnp.sum(p, axis=-1, keepdims=True), _dot(p.astype(BF16), v_own)]

    def body(j, carry):
        start = pl.multiple_of(j * tq, tq)
        kj = k_ref[0, pl.ds(start, tq), :]
        vj = v_ref[0, pl.ds(start, tq), :]
        out = []
        for c in range(2):
            m, l, acc = carry[3 * c:3 * c + 3]
            s = _dot_nt(qm[c], kj)
            m_new = jnp.maximum(m, jnp.max(s, axis=-1, keepdims=True))
            alpha = jnp.exp(m - m_new)
            p = jnp.exp(s - m_new)
            out += [m_new, alpha * l + jnp.sum(p, axis=-1, keepdims=True),
                    alpha * acc + _dot(p.astype(BF16), vj)]
        return tuple(out)

    carry = lax.fori_loop(0, qi, body, tuple(carry))
    lam = (jnp.exp(jnp.sum(lq1_ref[...] * lk1_ref[...], axis=-1, keepdims=True))
           - jnp.exp(jnp.sum(lq2_ref[...] * lk2_ref[...], axis=-1, keepdims=True)) + lambda_init)
    o = carry[2] / carry[1] - lam * (carry[5] / carry[4])
    o_ref[0] = (_rms_norm(o, g_ref[...]) * (1.0 - lambda_init)).astype(BF16)


def _diff_attn(q, k, v, lq1, lk1, lq2, lk2, subln_g, *, lambda_init, tq):
    b, s, width = q.shape
    qspec = pl.BlockSpec((1, tq, LANES), lambda bi, h, qi: (bi, qi, h))
    kvspec = pl.BlockSpec((1, s, LANES), lambda bi, h, qi: (bi, 0, h))
    return pl.pallas_call(
        functools.partial(_diff_kernel, lambda_init=lambda_init),
        grid=(b, width // LANES, s // tq),
        in_specs=[_resident()] * 5 + [qspec, kvspec, kvspec],
        out_specs=qspec,
        out_shape=jax.ShapeDtypeStruct((b, s, width), BF16),
        compiler_params=_params(3),
        name="diff_attn",
    )(lq1, lk1, lq2, lk2, subln_g, q, k, v)


def kernel(x, ffn_pre_norm, ffn_pre_w_gate, ffn_pre_w_up, ffn_pre_w_down, mix_norm, ffn_post_norm, ffn_post_w_gate, ffn_post_w_up, ffn_post_w_down, even_w_in, even_w_out, gmlp_ln_g, gmlp_ln_b, gmlp_w_s, gmlp_b_s, odd_w_in, odd_w_out, diff_lambda_q1, diff_lambda_k1, diff_lambda_q2, diff_lambda_k2, diff_subln_g, final_norm):
    b, s, d = x.shape
    depth = ffn_pre_norm.shape[0]
    t = b * s
    tm = 512
    tables = _rope_tables(s)
    xf = x.reshape(t, d)
    fin = final_norm.reshape(1, d)

    def ffn(xf, norm, wg, wu, wd, final):
        return _ffn(xf, norm.reshape(1, d), wg.astype(BF16), wu.astype(BF16), wd.astype(BF16), fin,
                    final_norm=final, tm=tm)

    for layer in range(depth):
        xf = ffn(xf, ffn_pre_norm[layer], ffn_pre_w_gate[layer], ffn_pre_w_up[layer], ffn_pre_w_down[layer], False)
        g_mix = mix_norm[layer].reshape(1, d)
        if layer % 2 == 0:
            e = layer // 2
            width = GMLP_GROUPS * GMLP_GROUP_DIM
            group = jnp.arange(width) // GMLP_GROUP_DIM
            avg = (group[:, None] == group[None, :]).astype(BF16) / GMLP_GROUP_DIM
            bias = jnp.repeat(gmlp_b_s[e].T, GMLP_GROUP_DIM, axis=1)
            q, k, v, km, gated = _even_in(
                xf, g_mix, even_w_in[e].astype(BF16), tables, avg, gmlp_ln_g[e].reshape(1, width),
                gmlp_ln_b[e].reshape(1, width), gmlp_w_s[e], bias, seq_len=s, tm=tm)
            attn = _moba(q.reshape(b, s, width), k.reshape(b, s, width), v.reshape(b, s, width),
                         km.reshape(b, s // MOBA_BLOCK, width))
            w_out = even_w_out[e].astype(BF16)
            xf = _out_proj(xf, [attn.reshape(t, width), gated], [w_out[:width], w_out[width:]], tm=tm)
        else:
            o = layer // 2
            lambda_init = 0.8 - 0.6 * math.exp(-0.3 * layer)
            v_width = odd_w_out.shape[1]
            qk_width = (odd_w_in.shape[2] - v_width) // 2
            q, k, v = _odd_in(xf, g_mix, odd_w_in[o].astype(BF16), tables, qk_width=qk_width, seq_len=s, tm=tm)
            vec = lambda a: a[o].reshape(1, -1)
            attn = _diff_attn(q.reshape(b, s, qk_width), k.reshape(b, s, qk_width), v.reshape(b, s, v_width),
                              vec(diff_lambda_q1), vec(diff_lambda_k1), vec(diff_lambda_q2), vec(diff_lambda_k2),
                              vec(diff_subln_g), lambda_init=lambda_init, tq=256)
            xf = _out_proj(xf, [attn.reshape(t, v_width)], [odd_w_out[o].astype(BF16)], tm=tm)
        xf = ffn(xf, ffn_post_norm[layer], ffn_post_w_gate[layer], ffn_post_w_up[layer], ffn_post_w_down[layer],
                 layer == depth - 1)
    return xf.reshape(b, s, d)
```

```python
import functools
import math

import jax
import jax.numpy as jnp
from jax import lax
from jax.experimental import pallas as pl
from jax.experimental.pallas import tpu as pltpu

F32 = jnp.float32
BF16 = jnp.bfloat16

HEAD_DIM = 64
ROT_DIM = HEAD_DIM // 4
ROPE_THETA = 500000.0
NORM_EPS = 1e-6
MOBA_BLOCK = 256
MOBA_TOPK = 3
GMLP_GROUPS = 8
GMLP_GROUP_DIM = 64
GMLP_CHUNK = 128

LANES = 128
VMEM_LIMIT_BYTES = 56 * 1024 * 1024
ATTN_TILE = 512

NEG_INF = float("-inf")
MASKED = -1e30
QK_SCALE = HEAD_DIM ** -0.5 * math.log2(math.e)


def _params(n_grid_axes):
    return pltpu.CompilerParams(
        dimension_semantics=("parallel",) * (n_grid_axes - 1) + ("arbitrary",),
        vmem_limit_bytes=VMEM_LIMIT_BYTES,
    )


def _resident():
    return pl.BlockSpec(memory_space=pltpu.VMEM)


def _rms_norm(x, g):
    ms = jnp.mean(x * x, axis=-1, keepdims=True)
    return x * lax.rsqrt(ms + NORM_EPS) * g


def _dot(a, b):
    return jnp.dot(a, b, preferred_element_type=F32)


def _dot_nt(a, b):
    return lax.dot_general(a, b, (((1,), (1,)), ((), ())), preferred_element_type=F32)


def _ffn_kernel(x_ref, g_ref, wg_ref, wu_ref, wd_ref, fin_ref, o_ref, *, final_norm):
    x = x_ref[...]
    h = _rms_norm(x, g_ref[...]).astype(BF16)
    gate = _dot(h, wg_ref[...])
    up = _dot(h, wu_ref[...])
    act = (gate * jax.nn.sigmoid(gate) * up).astype(BF16)
    y = x + 0.5 * _dot(act, wd_ref[...])
    if final_norm:
        y = _rms_norm(y, fin_ref[...])
    o_ref[...] = y


def _ffn(x, g, wg, wu, wd, fin, *, final_norm, tm):
    t, d = x.shape
    row = pl.BlockSpec((tm, d), lambda i: (i, 0))
    return pl.pallas_call(
        functools.partial(_ffn_kernel, final_norm=final_norm),
        grid=(t // tm,),
        in_specs=[row, _resident(), _resident(), _resident(), _resident(), _resident()],
        out_specs=row,
        out_shape=jax.ShapeDtypeStruct((t, d), F32),
        compiler_params=_params(1),
        name="ffn_final" if final_norm else "ffn",
    )(x, g, wg, wu, wd, fin)


def _rope_tables(seq_len):
    half = ROT_DIM // 2
    inv = 1.0 / (ROPE_THETA ** (jnp.arange(0, ROT_DIM, 2, dtype=F32) / ROT_DIM))
    ang = jnp.arange(seq_len, dtype=F32)[:, None] * inv[None, :]
    cos, sin = jnp.cos(ang), jnp.sin(ang)
    ones = jnp.ones((seq_len, HEAD_DIM - ROT_DIM), F32)
    zeros_h = jnp.zeros((seq_len, half), F32)
    zeros_r = jnp.zeros((seq_len, HEAD_DIM - ROT_DIM), F32)
    c_same = jnp.concatenate([cos, cos, ones], axis=-1)
    c_prev = jnp.concatenate([zeros_h, sin, zeros_r], axis=-1)
    c_next = jnp.concatenate([-sin, zeros_h, zeros_r], axis=-1)
    rep = LANES // HEAD_DIM
    return tuple(jnp.tile(c, (1, rep)) for c in (c_same, c_prev, c_next))


def _rope(zc, c_same, c_prev, c_next):
    half = ROT_DIM // 2
    return (zc * c_same + pltpu.roll(zc, half, 1) * c_prev
            + pltpu.roll(zc, LANES - half, 1) * c_next)


def _split_bf16(a):
    hi = a.astype(BF16)
    lo = (a - hi.astype(F32)).astype(BF16)
    return hi, lo


def _group_mean(a, avg):
    hi, lo = _split_bf16(a)
    return _dot(hi, avg) + _dot(lo, avg)


def _even_in_kernel(x_ref, g_ref, w_ref, cs_ref, cp_ref, cn_ref, avg_ref, lng_ref, lnb_ref, ws_ref,
                    bias_ref, q_ref, k_ref, v_ref, km_ref, gated_ref, *, width):
    tm = x_ref.shape[0]
    h = _rms_norm(x_ref[...], g_ref[...]).astype(BF16)
    z = _dot(h, w_ref[...])
    c_same, c_prev, c_next = cs_ref[...], cp_ref[...], cn_ref[...]
    n_lane_blocks = width // LANES
    for cb in range(n_lane_blocks):
        lo = cb * LANES
        q_ref[:, lo:lo + LANES] = (_rope(z[:, lo:lo + LANES], c_same, c_prev, c_next) * QK_SCALE).astype(BF16)
        kr = _rope(z[:, width + lo:width + lo + LANES], c_same, c_prev, c_next)
        k_ref[:, lo:lo + LANES] = kr.astype(BF16)
        for blk in range(tm // MOBA_BLOCK):
            rows = kr[blk * MOBA_BLOCK:(blk + 1) * MOBA_BLOCK]
            km_ref[blk, :, lo:lo + LANES] = jnp.mean(rows, axis=0, keepdims=True)
    v_ref[...] = z[:, 2 * width:3 * width].astype(BF16)

    gz = z[:, 3 * width:]
    gz = 0.5 * gz * (1.0 + jnp.tanh(math.sqrt(2.0 / math.pi) * (gz + 0.044715 * (gz * gz * gz))))
    u, vv = gz[:, :width], gz[:, width:]
    avg = avg_ref[...]
    cen = vv - _group_mean(vv, avg)
    var = _group_mean(cen * cen, avg)
    vn = (cen * lax.rsqrt(var + NORM_EPS) * lng_ref[...] + lnb_ref[...]).astype(BF16)

    t_idx = lax.broadcasted_iota(jnp.int32, (GMLP_CHUNK, GMLP_CHUNK), 0)
    s_idx = lax.broadcasted_iota(jnp.int32, (GMLP_CHUNK, GMLP_CHUNK), 1)
    causal = s_idx <= t_idx
    first_group = lax.broadcasted_iota(jnp.int32, (GMLP_CHUNK, LANES), 1) < GMLP_GROUP_DIM
    for pair in range(n_lane_blocks):
        lo = pair * LANES
        w0 = jnp.where(causal, ws_ref[2 * pair], 0.0).astype(BF16)
        w1 = jnp.where(causal, ws_ref[2 * pair + 1], 0.0).astype(BF16)
        bias = bias_ref[:, lo:lo + LANES]
        for c in range(tm // GMLP_CHUNK):
            r0 = c * GMLP_CHUNK
            vc = vn[r0:r0 + GMLP_CHUNK, lo:lo + LANES]
            mixed = jnp.where(first_group, _dot(w0, vc), _dot(w1, vc)) + bias
            gated_ref[r0:r0 + GMLP_CHUNK, lo:lo + LANES] = (u[r0:r0 + GMLP_CHUNK, lo:lo + LANES] * mixed).astype(BF16)


def _even_in(x, g, w_in, tables, avg, ln_g, ln_b, w_s, bias, *, seq_len, tm):
    t, d = x.shape
    width = w_in.shape[1] // 5
    n_seq_tiles = seq_len // tm
    row = pl.BlockSpec((tm, d), lambda i: (i, 0))
    tab = pl.BlockSpec((tm, LANES), lambda i: (i % n_seq_tiles, 0))
    out_row = pl.BlockSpec((tm, width), lambda i: (i, 0))
    nb = tm // MOBA_BLOCK
    act = jax.ShapeDtypeStruct((t, width), BF16)
    return pl.pallas_call(
        functools.partial(_even_in_kernel, width=width),
        grid=(t // tm,),
        in_specs=[row, _resident(), _resident(), tab, tab, tab, _resident(), _resident(), _resident(),
                  _resident(), _resident()],
        out_specs=[out_row, out_row, out_row, pl.BlockSpec((nb, 1, width), lambda i: (i, 0, 0)), out_row],
        out_shape=[act, act, act, jax.ShapeDtypeStruct((t // MOBA_BLOCK, 1, width), F32), act],
        compiler_params=_params(1),
        name="even_in",
    )(x, g, w_in, *tables, avg, ln_g, ln_b, w_s, bias)


def _stack_halves(q):
    lane = lax.broadcasted_iota(jnp.int32, q.shape, 1)
    zero = jnp.zeros_like(q)
    return jnp.concatenate([jnp.where(lane < HEAD_DIM, q, zero), jnp.where(lane >= HEAD_DIM, q, zero)], axis=0)


def _flash_scratch(rows, tk):
    state = pltpu.VMEM((rows, LANES), F32)
    return [pltpu.VMEM((rows, tk), F32), pltpu.VMEM((rows, tk), BF16), state, state, state, state]


def _softmax_stage(mask, s_ref, p_ref, alpha_ref, m_ref, l_ref):
    s = s_ref[...]
    if mask is not None:
        half = mask.shape[0]
        s = jnp.concatenate([jnp.where(mask, s[:half], NEG_INF), jnp.where(mask, s[half:], NEG_INF)], axis=0)
    cols = [s[:, c * LANES:(c + 1) * LANES] for c in range(s.shape[1] // LANES)]
    m_prev = m_ref[...]
    m_new = jnp.maximum(m_prev, jnp.max(functools.reduce(jnp.maximum, cols), axis=-1, keepdims=True))
    alpha = jnp.exp2(m_prev - m_new)
    ps = [jnp.exp2(c - m_new) for c in cols]
    l_ref[...] = alpha * l_ref[...] + functools.reduce(jnp.add, ps)
    p_ref[...] = jnp.concatenate([p.astype(BF16) for p in ps], axis=1)
    alpha_ref[...] = alpha
    m_ref[...] = m_new


def _pv_stage(v, p_ref, alpha_ref, acc_ref):
    acc_ref[...] = alpha_ref[...] * acc_ref[...] + _dot(p_ref[...], v)


def _flash_pipeline(n_past, qk_fn, v_fn, s_ref, p_ref, alpha_ref, m_ref, l_ref, acc_ref):
    m_ref[...] = jnp.full(m_ref.shape, NEG_INF, F32)
    l_ref[...] = jnp.zeros(l_ref.shape, F32)
    acc_ref[...] = jnp.zeros(acc_ref.shape, F32)
    p_ref[...] = jnp.zeros(p_ref.shape, BF16)
    alpha_ref[...] = jnp.ones(alpha_ref.shape, F32)
    s_ref[...] = qk_fn(0)

    def body(j, carry):
        _pv_stage(v_fn(jnp.maximum(j - 1, 0)), p_ref, alpha_ref, acc_ref)
        _softmax_stage(None, s_ref, p_ref, alpha_ref, m_ref, l_ref)
        s_ref[...] = qk_fn(j + 1)
        return carry

    lax.fori_loop(0, n_past, body, 0)
    _pv_stage(v_fn(jnp.maximum(n_past - 1, 0)), p_ref, alpha_ref, acc_ref)
    _softmax_stage(_causal_mask(s_ref.shape[1]), s_ref, p_ref, alpha_ref, m_ref, l_ref)
    _pv_stage(v_fn(n_past), p_ref, alpha_ref, acc_ref)
    return acc_ref[...] / jnp.sum(l_ref[...], axis=-1, keepdims=True)


def _causal_mask(tq):
    row = lax.broadcasted_iota(jnp.int32, (tq, tq), 0)
    col = lax.broadcasted_iota(jnp.int32, (tq, tq), 1)
    return col <= row


def _block_bias(gate, own_block, n_blocks):
    block_id = lax.broadcasted_iota(jnp.int32, gate.shape, 1)
    g = jnp.where(block_id < own_block, gate, NEG_INF)
    bias = jnp.where(block_id < n_blocks, jnp.where(block_id == own_block, 0.0, MASKED), 0.0)
    for _ in range(MOBA_TOPK):
        best = jnp.max(g, axis=-1, keepdims=True)
        cand = jnp.where(g == best, jnp.where(best > NEG_INF, block_id, LANES), LANES)
        pick = block_id == jnp.min(cand, axis=-1, keepdims=True)
        bias = jnp.where(pick, 0.0, bias)
        g = jnp.where(pick, NEG_INF, g)
    return bias


def _moba_kernel(q_ref, k_ref, v_ref, km_ref, o_ref, *scratch, n_blocks):
    tq = q_ref.shape[1]
    log2_block = MOBA_BLOCK.bit_length() - 1
    blocks_per_tile = tq // MOBA_BLOCK
    qi = pl.program_id(2)
    qs = _stack_halves(q_ref[0])
    gate = _dot_nt(qs, km_ref[0].astype(BF16))
    row = lax.broadcasted_iota(jnp.int32, gate.shape, 0)
    own_block = qi * blocks_per_tile + jnp.right_shift(jnp.bitwise_and(row, tq - 1), log2_block)
    q_aug = jnp.concatenate([qs, _block_bias(gate, own_block, n_blocks).astype(BF16)], axis=1)

    key_row = lax.broadcasted_iota(jnp.int32, (tq, LANES), 0)
    key_lane = lax.broadcasted_iota(jnp.int32, (tq, LANES), 1)
    key_block = jnp.right_shift(key_row, log2_block)

    def scores(j):
        one_hot = jnp.where(key_lane == j * blocks_per_tile + key_block, 1.0, 0.0).astype(BF16)
        k = k_ref[0, pl.ds(pl.multiple_of(j * tq, tq), tq), :]
        return _dot_nt(q_aug, jnp.concatenate([k, one_hot], axis=1))

    def values(j):
        return v_ref[0, pl.ds(pl.multiple_of(j * tq, tq), tq), :]

    o = _flash_pipeline(qi, scores, values, *scratch)
    o_ref[0] = jnp.where(key_lane < HEAD_DIM, o[:tq], o[tq:]).astype(BF16)


def _moba(q, k, v, km, *, tq):
    b, s, width = q.shape
    n_blocks = s // MOBA_BLOCK
    assert n_blocks <= LANES and tq % MOBA_BLOCK == 0 and tq & (tq - 1) == 0
    qspec = pl.BlockSpec((1, tq, LANES), lambda bi, hp, qi: (bi, qi, hp))
    kvspec = pl.BlockSpec((1, s, LANES), lambda bi, hp, qi: (bi, 0, hp))
    return pl.pallas_call(
        functools.partial(_moba_kernel, n_blocks=n_blocks),
        grid=(b, width // LANES, s // tq),
        in_specs=[qspec, kvspec, kvspec, pl.BlockSpec((1, LANES, LANES), lambda bi, hp, qi: (bi, 0, hp))],
        out_specs=qspec,
        out_shape=jax.ShapeDtypeStruct((b, s, width), BF16),
        scratch_shapes=_flash_scratch(2 * tq, tq),
        compiler_params=_params(3),
        name="moba_attn",
    )(q, k, v, km)


def _out_proj_kernel(*refs, n_in):
    x_ref, o_ref = refs[0], refs[-1]
    y = x_ref[...]
    for i in range(n_in):
        y = y + _dot(refs[1 + i][...], refs[1 + n_in + i][...])
    o_ref[...] = y


def _out_proj(x, acts, weights, *, tm):
    t, d = x.shape
    row = pl.BlockSpec((tm, d), lambda i: (i, 0))
    act_specs = [pl.BlockSpec((tm, a.shape[1]), lambda i: (i, 0)) for a in acts]
    return pl.pallas_call(
        functools.partial(_out_proj_kernel, n_in=len(acts)),
        grid=(t // tm,),
        in_specs=[row] + act_specs + [_resident()] * len(weights),
        out_specs=row,
        out_shape=jax.ShapeDtypeStruct((t, d), F32),
        compiler_params=_params(1),
        name="out_proj",
    )(x, *acts, *weights)


def _odd_in_kernel(x_ref, g_ref, w_ref, cs_ref, cp_ref, cn_ref, q_ref, k_ref, v_ref, *, qk_width):
    h = _rms_norm(x_ref[...], g_ref[...]).astype(BF16)
    z = _dot(h, w_ref[...])
    c_same, c_prev, c_next = cs_ref[...], cp_ref[...], cn_ref[...]
    for cb in range(qk_width // LANES):
        lo = cb * LANES
        q_ref[:, lo:lo + LANES] = (_rope(z[:, lo:lo + LANES], c_same, c_prev, c_next) * QK_SCALE).astype(BF16)
        k_ref[:, lo:lo + LANES] = _rope(z[:, qk_width + lo:qk_width + lo + LANES], c_same, c_prev, c_next).astype(BF16)
    v_ref[...] = z[:, 2 * qk_width:].astype(BF16)


def _odd_in(x, g, w_in, tables, *, qk_width, seq_len, tm):
    t, d = x.shape
    v_width = w_in.shape[1] - 2 * qk_width
    n_seq_tiles = seq_len // tm
    row = pl.BlockSpec((tm, d), lambda i: (i, 0))
    tab = pl.BlockSpec((tm, LANES), lambda i: (i % n_seq_tiles, 0))
    qk_row = pl.BlockSpec((tm, qk_width), lambda i: (i, 0))
    qk = jax.ShapeDtypeStruct((t, qk_width), BF16)
    return pl.pallas_call(
        functools.partial(_odd_in_kernel, qk_width=qk_width),
        grid=(t // tm,),
        in_specs=[row, _resident(), _resident(), tab, tab, tab],
        out_specs=[qk_row, qk_row, pl.BlockSpec((tm, v_width), lambda i: (i, 0))],
        out_shape=[qk, qk, jax.ShapeDtypeStruct((t, v_width), BF16)],
        compiler_params=_params(1),
        name="odd_in",
    )(x, g, w_in, *tables)


def _diff_kernel(lq1_ref, lk1_ref, lq2_ref, lk2_ref, g_ref, q_ref, k_ref, v_ref, o_ref, *scratch, lambda_init):
    tq = q_ref.shape[1]
    qs = _stack_halves(q_ref[0])

    def scores(j):
        return _dot_nt(qs, k_ref[0, pl.ds(pl.multiple_of(j * tq, tq), tq), :])

    def values(j):
        return v_ref[0, pl.ds(pl.multiple_of(j * tq, tq), tq), :]

    o = _flash_pipeline(pl.program_id(2), scores, values, *scratch)
    lam = (jnp.exp(jnp.sum(lq1_ref[...] * lk1_ref[...], axis=-1, keepdims=True))
           - jnp.exp(jnp.sum(lq2_ref[...] * lk2_ref[...], axis=-1, keepdims=True)) + lambda_init)
    o = o[:tq] - lam * o[tq:]
    o_ref[0] = (_rms_norm(o, g_ref[...]) * (1.0 - lambda_init)).astype(BF16)


def _diff_attn(q, k, v, lq1, lk1, lq2, lk2, subln_g, *, lambda_init, tq):
    b, s, width = q.shape
    qspec = pl.BlockSpec((1, tq, LANES), lambda bi, h, qi: (bi, qi, h))
    kvspec = pl.BlockSpec((1, s, LANES), lambda bi, h, qi: (bi, 0, h))
    return pl.pallas_call(
        functools.partial(_diff_kernel, lambda_init=lambda_init),
        grid=(b, width // LANES, s // tq),
        in_specs=[_resident()] * 5 + [qspec, kvspec, kvspec],
        out_specs=qspec,
        out_shape=jax.ShapeDtypeStruct((b, s, width), BF16),
        scratch_shapes=_flash_scratch(2 * tq, tq),
        compiler_params=_params(3),
        name="diff_attn",
    )(lq1, lk1, lq2, lk2, subln_g, q, k, v)


def kernel(x, ffn_pre_norm, ffn_pre_w_gate, ffn_pre_w_up, ffn_pre_w_down, mix_norm, ffn_post_norm, ffn_post_w_gate, ffn_post_w_up, ffn_post_w_down, even_w_in, even_w_out, gmlp_ln_g, gmlp_ln_b, gmlp_w_s, gmlp_b_s, odd_w_in, odd_w_out, diff_lambda_q1, diff_lambda_k1, diff_lambda_q2, diff_lambda_k2, diff_subln_g, final_norm):
    b, s, d = x.shape
    depth = ffn_pre_norm.shape[0]
    t = b * s
    tm = 512
    tables = _rope_tables(s)
    xf = x.reshape(t, d)
    fin = final_norm.reshape(1, d)

    def ffn(xf, norm, wg, wu, wd, final):
        return _ffn(xf, norm.reshape(1, d), wg.astype(BF16), wu.astype(BF16), wd.astype(BF16), fin,
                    final_norm=final, tm=tm)

    for layer in range(depth):
        xf = ffn(xf, ffn_pre_norm[layer], ffn_pre_w_gate[layer], ffn_pre_w_up[layer], ffn_pre_w_down[layer], False)
        g_mix = mix_norm[layer].reshape(1, d)
        if layer % 2 == 0:
            e = layer // 2
            width = GMLP_GROUPS * GMLP_GROUP_DIM
            n_blocks = s // MOBA_BLOCK
            group = jnp.arange(width) // GMLP_GROUP_DIM
            avg = (group[:, None] == group[None, :]).astype(BF16) / GMLP_GROUP_DIM
            bias = jnp.repeat(gmlp_b_s[e].T, GMLP_GROUP_DIM, axis=1)
            q, k, v, km, gated = _even_in(
                xf, g_mix, even_w_in[e].astype(BF16), tables, avg, gmlp_ln_g[e].reshape(1, width),
                gmlp_ln_b[e].reshape(1, width), gmlp_w_s[e], bias, seq_len=s, tm=tm)
            km = jnp.pad(km.reshape(b, n_blocks, width), ((0, 0), (0, LANES - n_blocks), (0, 0)))
            attn = _moba(q.reshape(b, s, width), k.reshape(b, s, width), v.reshape(b, s, width), km,
                         tq=ATTN_TILE)
            w_out = even_w_out[e].astype(BF16)
            xf = _out_proj(xf, [attn.reshape(t, width), gated], [w_out[:width], w_out[width:]], tm=tm)
        else:
            o = layer // 2
            lambda_init = 0.8 - 0.6 * math.exp(-0.3 * layer)
            v_width = odd_w_out.shape[1]
            qk_width = (odd_w_in.shape[2] - v_width) // 2
            q, k, v = _odd_in(xf, g_mix, odd_w_in[o].astype(BF16), tables, qk_width=qk_width, seq_len=s, tm=tm)
            vec = lambda a: a[o].reshape(1, -1)
            attn = _diff_attn(q.reshape(b, s, qk_width), k.reshape(b, s, qk_width), v.reshape(b, s, v_width),
                              vec(diff_lambda_q1), vec(diff_lambda_k1), vec(diff_lambda_q2), vec(diff_lambda_k2),
                              vec(diff_subln_g), lambda_init=lambda_init, tq=ATTN_TILE)
            xf = _out_proj(xf, [attn.reshape(t, v_width)], [odd_w_out[o].astype(BF16)], tm=tm)
        xf = ffn(xf, ffn_post_norm[layer], ffn_post_w_gate[layer], ffn_post_w_up[layer], ffn_post_w_down[layer],
                 layer == depth - 1)
    return xf.reshape(b, s, d)
```

```python
import functools
import math

import jax
import jax.numpy as jnp
from jax import lax
from jax.experimental import pallas as pl
from jax.experimental.pallas import tpu as pltpu

F32 = jnp.float32
BF16 = jnp.bfloat16

HEAD_DIM = 64
ROT_DIM = HEAD_DIM // 4
ROPE_THETA = 500000.0
NORM_EPS = 1e-6
MOBA_BLOCK = 256
MOBA_TOPK = 3
GMLP_GROUPS = 8
GMLP_GROUP_DIM = 64
GMLP_CHUNK = 128

LANES = 128
VMEM_LIMIT_BYTES = 56 * 1024 * 1024
ATTN_Q_TILE = 1024
ATTN_KV_TILE = 512

NEG_INF = float("-inf")
MASKED = -1e30
QK_SCALE = HEAD_DIM ** -0.5 * math.log2(math.e)


def _params(n_grid_axes):
    return pltpu.CompilerParams(
        dimension_semantics=("parallel",) * (n_grid_axes - 1) + ("arbitrary",),
        vmem_limit_bytes=VMEM_LIMIT_BYTES,
    )


def _resident():
    return pl.BlockSpec(memory_space=pltpu.VMEM)


def _rms_norm(x, g):
    ms = jnp.mean(x * x, axis=-1, keepdims=True)
    return x * lax.rsqrt(ms + NORM_EPS) * g


def _dot(a, b):
    return jnp.dot(a, b, preferred_element_type=F32)


def _dot_nt(a, b):
    return lax.dot_general(a, b, (((1,), (1,)), ((), ())), preferred_element_type=F32)


def _ffn_kernel(*refs, n_mix, final_norm):
    x_ref, mix_refs, mix_w_refs = refs[0], refs[1:1 + n_mix], refs[1 + n_mix:1 + 2 * n_mix]
    g_ref, wg_ref, wu_ref, wd_ref, fin_ref, o_ref = refs[1 + 2 * n_mix:]
    x = x_ref[...]
    for a_ref, w_ref in zip(mix_refs, mix_w_refs):
        x = x + _dot(a_ref[...], w_ref[...])
    h = _rms_norm(x, g_ref[...]).astype(BF16)
    gate = _dot(h, wg_ref[...])
    up = _dot(h, wu_ref[...])
    act = (gate * jax.nn.sigmoid(gate) * up).astype(BF16)
    y = x + 0.5 * _dot(act, wd_ref[...])
    if final_norm:
        y = _rms_norm(y, fin_ref[...])
    o_ref[...] = y


def _ffn(x, mix, mix_w, g, wg, wu, wd, fin, *, final_norm, tm):
    t, d = x.shape
    row = pl.BlockSpec((tm, d), lambda i: (i, 0))
    mix_specs = [pl.BlockSpec((tm, a.shape[1]), lambda i: (i, 0)) for a in mix]
    return pl.pallas_call(
        functools.partial(_ffn_kernel, n_mix=len(mix), final_norm=final_norm),
        grid=(t // tm,),
        in_specs=[row] + mix_specs + [_resident()] * (len(mix_w) + 5),
        out_specs=row,
        out_shape=jax.ShapeDtypeStruct((t, d), F32),
        compiler_params=_params(1),
        name="ffn_final" if final_norm else ("ffn_mix" if mix else "ffn"),
    )(x, *mix, *mix_w, g, wg, wu, wd, fin)


def _rope_tables(seq_len):
    half = ROT_DIM // 2
    inv = 1.0 / (ROPE_THETA ** (jnp.arange(0, ROT_DIM, 2, dtype=F32) / ROT_DIM))
    ang = jnp.arange(seq_len, dtype=F32)[:, None] * inv[None, :]
    cos, sin = jnp.cos(ang), jnp.sin(ang)
    ones = jnp.ones((seq_len, HEAD_DIM - ROT_DIM), F32)
    zeros_h = jnp.zeros((seq_len, half), F32)
    zeros_r = jnp.zeros((seq_len, HEAD_DIM - ROT_DIM), F32)
    c_same = jnp.concatenate([cos, cos, ones], axis=-1)
    c_prev = jnp.concatenate([zeros_h, sin, zeros_r], axis=-1)
    c_next = jnp.concatenate([-sin, zeros_h, zeros_r], axis=-1)
    rep = LANES // HEAD_DIM
    return tuple(jnp.tile(c, (1, rep)) for c in (c_same, c_prev, c_next))


def _rope(zc, c_same, c_prev, c_next):
    half = ROT_DIM // 2
    return (zc * c_same + pltpu.roll(zc, half, 1) * c_prev
            + pltpu.roll(zc, LANES - half, 1) * c_next)


def _split_bf16(a):
    hi = a.astype(BF16)
    lo = (a - hi.astype(F32)).astype(BF16)
    return hi, lo


def _group_mean(a, avg):
    hi, lo = _split_bf16(a)
    return _dot(hi, avg) + _dot(lo, avg)


def _even_in_kernel(x_ref, g_ref, w_ref, cs_ref, cp_ref, cn_ref, avg_ref, lng_ref, lnb_ref, ws_ref,
                    bias_ref, q_ref, k_ref, v_ref, km_ref, gated_ref, *, width):
    tm = x_ref.shape[0]
    h = _rms_norm(x_ref[...], g_ref[...]).astype(BF16)
    z = _dot(h, w_ref[...])
    c_same, c_prev, c_next = cs_ref[...], cp_ref[...], cn_ref[...]
    n_lane_blocks = width // LANES
    for cb in range(n_lane_blocks):
        lo = cb * LANES
        q_ref[:, lo:lo + LANES] = (_rope(z[:, lo:lo + LANES], c_same, c_prev, c_next) * QK_SCALE).astype(BF16)
        kr = _rope(z[:, width + lo:width + lo + LANES], c_same, c_prev, c_next)
        k_ref[:, lo:lo + LANES] = kr.astype(BF16)
        for blk in range(tm // MOBA_BLOCK):
            rows = kr[blk * MOBA_BLOCK:(blk + 1) * MOBA_BLOCK]
            km_ref[blk, :, lo:lo + LANES] = jnp.mean(rows, axis=0, keepdims=True)
    v_ref[...] = z[:, 2 * width:3 * width].astype(BF16)

    gz = z[:, 3 * width:]
    gz = 0.5 * gz * (1.0 + jnp.tanh(math.sqrt(2.0 / math.pi) * (gz + 0.044715 * (gz * gz * gz))))
    u, vv = gz[:, :width], gz[:, width:]
    avg = avg_ref[...]
    cen = vv - _group_mean(vv, avg)
    var = _group_mean(cen * cen, avg)
    vn = (cen * lax.rsqrt(var + NORM_EPS) * lng_ref[...] + lnb_ref[...]).astype(BF16)

    t_idx = lax.broadcasted_iota(jnp.int32, (GMLP_CHUNK, GMLP_CHUNK), 0)
    s_idx = lax.broadcasted_iota(jnp.int32, (GMLP_CHUNK, GMLP_CHUNK), 1)
    causal = s_idx <= t_idx
    first_group = lax.broadcasted_iota(jnp.int32, (GMLP_CHUNK, LANES), 1) < GMLP_GROUP_DIM
    for pair in range(n_lane_blocks):
        lo = pair * LANES
        w0 = jnp.where(causal, ws_ref[2 * pair], 0.0).astype(BF16)
        w1 = jnp.where(causal, ws_ref[2 * pair + 1], 0.0).astype(BF16)
        bias = bias_ref[:, lo:lo + LANES]
        for c in range(tm // GMLP_CHUNK):
            r0 = c * GMLP_CHUNK
            vc = vn[r0:r0 + GMLP_CHUNK, lo:lo + LANES]
            mixed = jnp.where(first_group, _dot(w0, vc), _dot(w1, vc)) + bias
            gated_ref[r0:r0 + GMLP_CHUNK, lo:lo + LANES] = (u[r0:r0 + GMLP_CHUNK, lo:lo + LANES] * mixed).astype(BF16)


def _even_in(x, g, w_in, tables, avg, ln_g, ln_b, w_s, bias, *, seq_len, tm):
    t, d = x.shape
    width = w_in.shape[1] // 5
    n_seq_tiles = seq_len // tm
    row = pl.BlockSpec((tm, d), lambda i: (i, 0))
    tab = pl.BlockSpec((tm, LANES), lambda i: (i % n_seq_tiles, 0))
    out_row = pl.BlockSpec((tm, width), lambda i: (i, 0))
    nb = tm // MOBA_BLOCK
    act = jax.ShapeDtypeStruct((t, width), BF16)
    return pl.pallas_call(
        functools.partial(_even_in_kernel, width=width),
        grid=(t // tm,),
        in_specs=[row, _resident(), _resident(), tab, tab, tab, _resident(), _resident(), _resident(),
                  _resident(), _resident()],
        out_specs=[out_row, out_row, out_row, pl.BlockSpec((nb, 1, width), lambda i: (i, 0, 0)), out_row],
        out_shape=[act, act, act, jax.ShapeDtypeStruct((t // MOBA_BLOCK, 1, width), F32), act],
        compiler_params=_params(1),
        name="even_in",
    )(x, g, w_in, *tables, avg, ln_g, ln_b, w_s, bias)


def _stack_halves(q):
    lane = lax.broadcasted_iota(jnp.int32, q.shape, 1)
    zero = jnp.zeros_like(q)
    return jnp.concatenate([jnp.where(lane < HEAD_DIM, q, zero), jnp.where(lane >= HEAD_DIM, q, zero)], axis=0)


def _flash_scratch(rows, tk):
    scores, probs = pltpu.VMEM((rows, tk), F32), pltpu.VMEM((rows, tk), BF16)
    state = pltpu.VMEM((rows, LANES), F32)
    return [scores, scores, probs, probs, state, state, state, state, state]


def _softmax_stage(mask, s_ref, p_ref, alpha_ref, m_ref, l_ref):
    s = s_ref[...]
    if mask is not None:
        half = mask.shape[0]
        s = jnp.concatenate([jnp.where(mask, s[:half], NEG_INF), jnp.where(mask, s[half:], NEG_INF)], axis=0)
    cols = [s[:, c * LANES:(c + 1) * LANES] for c in range(s.shape[1] // LANES)]
    m_prev = m_ref[...]
    m_new = jnp.maximum(m_prev, jnp.max(functools.reduce(jnp.maximum, cols), axis=-1, keepdims=True))
    alpha = jnp.exp2(m_prev - m_new)
    ps = [jnp.exp2(c - m_new) for c in cols]
    l_ref[...] = alpha * l_ref[...] + functools.reduce(jnp.add, ps)
    p_ref[...] = jnp.concatenate([p.astype(BF16) for p in ps], axis=1)
    alpha_ref[...] = alpha
    m_ref[...] = m_new


def _pv_stage(v, p_ref, alpha_ref, acc_ref):
    acc_ref[...] = alpha_ref[...] * acc_ref[...] + _dot(p_ref[...], v)


def _flash_pipeline(qi, tq, qk_fn, v_fn, s0, s1, p0, p1, a0, a1, m_ref, l_ref, acc_ref):
    s_refs, p_refs, a_refs = (s0, s1), (p0, p1), (a0, a1)
    tk = s0.shape[1]
    tiles_per_q = tq // tk
    assert tiles_per_q % 2 == 0
    n_past = qi * tiles_per_q

    def tick(t, slot, mask, last=False):
        _pv_stage(v_fn(jnp.maximum(t - 1, 0)), p_refs[1 - slot], a_refs[1 - slot], acc_ref)
        _softmax_stage(mask, s_refs[slot], p_refs[slot], a_refs[slot], m_ref, l_ref)
        if not last:
            s_refs[1 - slot][...] = qk_fn(t + 1)

    m_ref[...] = jnp.full(m_ref.shape, NEG_INF, F32)
    l_ref[...] = jnp.zeros(l_ref.shape, F32)
    acc_ref[...] = jnp.zeros(acc_ref.shape, F32)
    p1[...] = jnp.zeros(p1.shape, BF16)
    a1[...] = jnp.ones(a1.shape, F32)
    s0[...] = qk_fn(0)

    def body(i, carry):
        tick(2 * i, 0, None)
        tick(2 * i + 1, 1, None)
        return carry

    lax.fori_loop(0, qi * (tiles_per_q // 2), body, 0)
    for d in range(tiles_per_q):
        tick(n_past + d, d % 2, _causal_mask(tq, tk, d * tk), last=d == tiles_per_q - 1)
    _pv_stage(v_fn(n_past + tiles_per_q - 1), p_refs[(tiles_per_q - 1) % 2], a_refs[(tiles_per_q - 1) % 2], acc_ref)
    return acc_ref[...] / jnp.sum(l_ref[...], axis=-1, keepdims=True)


def _causal_mask(tq, tk, key_offset):
    row = lax.broadcasted_iota(jnp.int32, (tq, tk), 0)
    col = lax.broadcasted_iota(jnp.int32, (tq, tk), 1)
    return col + key_offset <= row


def _block_bias(gate, own_block, n_blocks):
    block_id = lax.broadcasted_iota(jnp.int32, gate.shape, 0)
    g = jnp.where(block_id < own_block, gate, NEG_INF)
    bias = jnp.where(block_id < n_blocks, jnp.where(block_id == own_block, 0.0, MASKED), 0.0)
    for _ in range(MOBA_TOPK):
        best = jnp.max(g, axis=0, keepdims=True)
        cand = jnp.where(g == best, jnp.where(best > NEG_INF, block_id, LANES), LANES)
        pick = block_id == jnp.min(cand, axis=0, keepdims=True)
        bias = jnp.where(pick, 0.0, bias)
        g = jnp.where(pick, NEG_INF, g)
    return bias


def _moba_kernel(q_ref, k_ref, v_ref, km_ref, o_ref, *scratch, n_blocks):
    tq = q_ref.shape[1]
    log2_block = MOBA_BLOCK.bit_length() - 1
    blocks_per_tile = tq // MOBA_BLOCK
    qi = pl.program_id(2)
    qs = _stack_halves(q_ref[0])
    n_pad = -(-n_blocks // 8) * 8
    gate = _dot_nt(km_ref[0, :n_pad, :].astype(BF16), qs)
    query = jnp.bitwise_and(lax.broadcasted_iota(jnp.int32, gate.shape, 1), tq - 1)
    own_block = qi * blocks_per_tile + jnp.right_shift(query, log2_block)
    bias = _block_bias(gate, own_block, n_blocks)
    if n_pad < LANES:
        bias = jnp.concatenate([bias, jnp.zeros((LANES - n_pad, 2 * tq), F32)], axis=0)
    bias = bias.T
    q_aug = jnp.concatenate([qs, bias.astype(BF16)], axis=1)

    tk = scratch[0].shape[1]
    key_lane = lax.broadcasted_iota(jnp.int32, (tk, LANES), 1)
    key_block = jnp.right_shift(lax.broadcasted_iota(jnp.int32, (tk, LANES), 0), log2_block)

    def scores(t):
        one_hot = jnp.where(key_lane == t * (tk // MOBA_BLOCK) + key_block, 1.0, 0.0).astype(BF16)
        k = k_ref[0, pl.ds(pl.multiple_of(t * tk, tk), tk), :]
        return _dot_nt(q_aug, jnp.concatenate([k, one_hot], axis=1))

    def values(t):
        return v_ref[0, pl.ds(pl.multiple_of(t * tk, tk), tk), :]

    o = _flash_pipeline(qi, tq, scores, values, *scratch)
    out_lane = lax.broadcasted_iota(jnp.int32, (tq, LANES), 1)
    o_ref[0] = jnp.where(out_lane < HEAD_DIM, o[:tq], o[tq:]).astype(BF16)


def _moba(q, k, v, km, *, tq, tk):
    b, s, width = q.shape
    n_blocks = s // MOBA_BLOCK
    assert n_blocks <= LANES and tk % MOBA_BLOCK == 0 and tq & (tq - 1) == 0
    qspec = pl.BlockSpec((1, tq, LANES), lambda bi, hp, qi: (bi, qi, hp))
    kvspec = pl.BlockSpec((1, s, LANES), lambda bi, hp, qi: (bi, 0, hp))
    return pl.pallas_call(
        functools.partial(_moba_kernel, n_blocks=n_blocks),
        grid=(b, width // LANES, s // tq),
        in_specs=[qspec, kvspec, kvspec, pl.BlockSpec((1, LANES, LANES), lambda bi, hp, qi: (bi, 0, hp))],
        out_specs=qspec,
        out_shape=jax.ShapeDtypeStruct((b, s, width), BF16),
        scratch_shapes=_flash_scratch(2 * tq, tk),
        compiler_params=_params(3),
        name="moba_attn",
    )(q, k, v, km)


def _odd_in_kernel(x_ref, g_ref, w_ref, cs_ref, cp_ref, cn_ref, q_ref, k_ref, v_ref, *, qk_width):
    h = _rms_norm(x_ref[...], g_ref[...]).astype(BF16)
    z = _dot(h, w_ref[...])
    c_same, c_prev, c_next = cs_ref[...], cp_ref[...], cn_ref[...]
    for cb in range(qk_width // LANES):
        lo = cb * LANES
        q_ref[:, lo:lo + LANES] = (_rope(z[:, lo:lo + LANES], c_same, c_prev, c_next) * QK_SCALE).astype(BF16)
        k_ref[:, lo:lo + LANES] = _rope(z[:, qk_width + lo:qk_width + lo + LANES], c_same, c_prev, c_next).astype(BF16)
    v_ref[...] = z[:, 2 * qk_width:].astype(BF16)


def _odd_in(x, g, w_in, tables, *, qk_width, seq_len, tm):
    t, d = x.shape
    v_width = w_in.shape[1] - 2 * qk_width
    n_seq_tiles = seq_len // tm
    row = pl.BlockSpec((tm, d), lambda i: (i, 0))
    tab = pl.BlockSpec((tm, LANES), lambda i: (i % n_seq_tiles, 0))
    qk_row = pl.BlockSpec((tm, qk_width), lambda i: (i, 0))
    qk = jax.ShapeDtypeStruct((t, qk_width), BF16)
    return pl.pallas_call(
        functools.partial(_odd_in_kernel, qk_width=qk_width),
        grid=(t // tm,),
        in_specs=[row, _resident(), _resident(), tab, tab, tab],
        out_specs=[qk_row, qk_row, pl.BlockSpec((tm, v_width), lambda i: (i, 0))],
        out_shape=[qk, qk, jax.ShapeDtypeStruct((t, v_width), BF16)],
        compiler_params=_params(1),
        name="odd_in",
    )(x, g, w_in, *tables)


def _diff_kernel(lq1_ref, lk1_ref, lq2_ref, lk2_ref, g_ref, q_ref, k_ref, v_ref, o_ref, *scratch, lambda_init):
    tq = q_ref.shape[1]
    qs = _stack_halves(q_ref[0])

    tk = scratch[0].shape[1]

    def scores(t):
        return _dot_nt(qs, k_ref[0, pl.ds(pl.multiple_of(t * tk, tk), tk), :])

    def values(t):
        return v_ref[0, pl.ds(pl.multiple_of(t * tk, tk), tk), :]

    o = _flash_pipeline(pl.program_id(2), tq, scores, values, *scratch)
    lam = (jnp.exp(jnp.sum(lq1_ref[...] * lk1_ref[...], axis=-1, keepdims=True))
           - jnp.exp(jnp.sum(lq2_ref[...] * lk2_ref[...], axis=-1, keepdims=True)) + lambda_init)
    o = o[:tq] - lam * o[tq:]
    o_ref[0] = (_rms_norm(o, g_ref[...]) * (1.0 - lambda_init)).astype(BF16)


def _diff_attn(q, k, v, lq1, lk1, lq2, lk2, subln_g, *, lambda_init, tq, tk):
    b, s, width = q.shape
    qspec = pl.BlockSpec((1, tq, LANES), lambda bi, h, qi: (bi, qi, h))
    kvspec = pl.BlockSpec((1, s, LANES), lambda bi, h, qi: (bi, 0, h))
    return pl.pallas_call(
        functools.partial(_diff_kernel, lambda_init=lambda_init),
        grid=(b, width // LANES, s // tq),
        in_specs=[_resident()] * 5 + [qspec, kvspec, kvspec],
        out_specs=qspec,
        out_shape=jax.ShapeDtypeStruct((b, s, width), BF16),
        scratch_shapes=_flash_scratch(2 * tq, tk),
        compiler_params=_params(3),
        name="diff_attn",
    )(lq1, lk1, lq2, lk2, subln_g, q, k, v)


def kernel(x, ffn_pre_norm, ffn_pre_w_gate, ffn_pre_w_up, ffn_pre_w_down, mix_norm, ffn_post_norm, ffn_post_w_gate, ffn_post_w_up, ffn_post_w_down, even_w_in, even_w_out, gmlp_ln_g, gmlp_ln_b, gmlp_w_s, gmlp_b_s, odd_w_in, odd_w_out, diff_lambda_q1, diff_lambda_k1, diff_lambda_q2, diff_lambda_k2, diff_subln_g, final_norm):
    b, s, d = x.shape
    depth = ffn_pre_norm.shape[0]
    t = b * s
    tm = 512
    tables = _rope_tables(s)
    xf = x.reshape(t, d)
    fin = final_norm.reshape(1, d)

    def ffn(xf, mix, mix_w, norm, wg, wu, wd, final):
        return _ffn(xf, mix, mix_w, norm.reshape(1, d), wg.astype(BF16), wu.astype(BF16), wd.astype(BF16), fin,
                    final_norm=final, tm=tm)

    for layer in range(depth):
        xf = ffn(xf, [], [], ffn_pre_norm[layer], ffn_pre_w_gate[layer], ffn_pre_w_up[layer],
                 ffn_pre_w_down[layer], False)
        g_mix = mix_norm[layer].reshape(1, d)
        if layer % 2 == 0:
            e = layer // 2
            width = GMLP_GROUPS * GMLP_GROUP_DIM
            n_blocks = s // MOBA_BLOCK
            group = jnp.arange(width) // GMLP_GROUP_DIM
            avg = (group[:, None] == group[None, :]).astype(BF16) / GMLP_GROUP_DIM
            bias = jnp.repeat(gmlp_b_s[e].T, GMLP_GROUP_DIM, axis=1)
            q, k, v, km, gated = _even_in(
                xf, g_mix, even_w_in[e].astype(BF16), tables, avg, gmlp_ln_g[e].reshape(1, width),
                gmlp_ln_b[e].reshape(1, width), gmlp_w_s[e], bias, seq_len=s, tm=tm)
            km = jnp.pad(km.reshape(b, n_blocks, width), ((0, 0), (0, LANES - n_blocks), (0, 0)))
            attn = _moba(q.reshape(b, s, width), k.reshape(b, s, width), v.reshape(b, s, width), km,
                         tq=min(ATTN_Q_TILE, s), tk=ATTN_KV_TILE)
            w_out = even_w_out[e].astype(BF16)
            mix, mix_w = [attn.reshape(t, width), gated], [w_out[:width], w_out[width:]]
        else:
            o = layer // 2
            lambda_init = 0.8 - 0.6 * math.exp(-0.3 * layer)
            v_width = odd_w_out.shape[1]
            qk_width = (odd_w_in.shape[2] - v_width) // 2
            q, k, v = _odd_in(xf, g_mix, odd_w_in[o].astype(BF16), tables, qk_width=qk_width, seq_len=s, tm=tm)
            vec = lambda a: a[o].reshape(1, -1)
            attn = _diff_attn(q.reshape(b, s, qk_width), k.reshape(b, s, qk_width), v.reshape(b, s, v_width),
                              vec(diff_lambda_q1), vec(diff_lambda_k1), vec(diff_lambda_q2), vec(diff_lambda_k2),
                              vec(diff_subln_g), lambda_init=lambda_init, tq=min(ATTN_Q_TILE, s),
                              tk=ATTN_KV_TILE)
            mix, mix_w = [attn.reshape(t, v_width)], [odd_w_out[o].astype(BF16)]
        xf = ffn(xf, mix, mix_w, ffn_post_norm[layer], ffn_post_w_gate[layer], ffn_post_w_up[layer],
                 ffn_post_w_down[layer], layer == depth - 1)
    return xf.reshape(b, s, d)
```

```python
import functools
import math

import jax
import jax.numpy as jnp
from jax import lax
from jax.experimental import pallas as pl
from jax.experimental.pallas import tpu as pltpu

F32 = jnp.float32
BF16 = jnp.bfloat16

HEAD_DIM = 64
ROT_DIM = HEAD_DIM // 4
ROPE_THETA = 500000.0
NORM_EPS = 1e-6
MOBA_BLOCK = 256
MOBA_TOPK = 3
GMLP_GROUPS = 8
GMLP_GROUP_DIM = 64
GMLP_CHUNK = 128

LANES = 128
VMEM_LIMIT_BYTES = 56 * 1024 * 1024
ATTN_Q_TILE = 1024
ATTN_KV_TILE = 512

NEG_INF = float("-inf")
MASKED = -1e30
QK_SCALE = HEAD_DIM ** -0.5 * math.log2(math.e)


def _params(n_grid_axes):
    return pltpu.CompilerParams(
        dimension_semantics=("parallel",) * (n_grid_axes - 1) + ("arbitrary",),
        vmem_limit_bytes=VMEM_LIMIT_BYTES,
    )


def _resident():
    return pl.BlockSpec(memory_space=pltpu.VMEM)


def _rms_norm(x, g):
    ms = jnp.mean(x * x, axis=-1, keepdims=True)
    return x * lax.rsqrt(ms + NORM_EPS) * g


def _dot(a, b):
    return jnp.dot(a, b, preferred_element_type=F32)


def _dot_nt(a, b):
    return lax.dot_general(a, b, (((1,), (1,)), ((), ())), preferred_element_type=F32)


def _ffn_kernel(*refs, n_mix, final_norm):
    x_ref, mix_refs, mix_w_refs = refs[0], refs[1:1 + n_mix], refs[1 + n_mix:1 + 2 * n_mix]
    g_ref, wg_ref, wu_ref, wd_ref, fin_ref, o_ref = refs[1 + 2 * n_mix:]
    x = x_ref[...]
    for a_ref, w_ref in zip(mix_refs, mix_w_refs):
        x = x + _dot(a_ref[...], w_ref[...])
    h = _rms_norm(x, g_ref[...]).astype(BF16)
    gate = _dot(h, wg_ref[...])
    up = _dot(h, wu_ref[...])
    act = (gate * jax.nn.sigmoid(gate) * up).astype(BF16)
    y = x + 0.5 * _dot(act, wd_ref[...])
    if final_norm:
        y = _rms_norm(y, fin_ref[...])
    o_ref[...] = y


def _ffn(x, mix, mix_w, g, wg, wu, wd, fin, *, final_norm, tm):
    t, d = x.shape
    row = pl.BlockSpec((tm, d), lambda i: (i, 0))
    mix_specs = [pl.BlockSpec((tm, a.shape[1]), lambda i: (i, 0)) for a in mix]
    return pl.pallas_call(
        functools.partial(_ffn_kernel, n_mix=len(mix), final_norm=final_norm),
        grid=(t // tm,),
        in_specs=[row] + mix_specs + [_resident()] * (len(mix_w) + 5),
        out_specs=row,
        out_shape=jax.ShapeDtypeStruct((t, d), F32),
        compiler_params=_params(1),
        name="ffn_final" if final_norm else ("ffn_mix" if mix else "ffn"),
    )(x, *mix, *mix_w, g, wg, wu, wd, fin)


def _rope_tables(seq_len):
    half = ROT_DIM // 2
    inv = 1.0 / (ROPE_THETA ** (jnp.arange(0, ROT_DIM, 2, dtype=F32) / ROT_DIM))
    ang = jnp.arange(seq_len, dtype=F32)[:, None] * inv[None, :]
    cos, sin = jnp.cos(ang), jnp.sin(ang)
    ones = jnp.ones((seq_len, HEAD_DIM - ROT_DIM), F32)
    zeros_h = jnp.zeros((seq_len, half), F32)
    zeros_r = jnp.zeros((seq_len, HEAD_DIM - ROT_DIM), F32)
    c_same = jnp.concatenate([cos, cos, ones], axis=-1)
    c_prev = jnp.concatenate([zeros_h, sin, zeros_r], axis=-1)
    c_next = jnp.concatenate([-sin, zeros_h, zeros_r], axis=-1)
    rep = LANES // HEAD_DIM
    return tuple(jnp.tile(c, (1, rep)) for c in (c_same, c_prev, c_next))


def _rope(zc, c_same, c_prev, c_next):
    half = ROT_DIM // 2
    return (zc * c_same + pltpu.roll(zc, half, 1) * c_prev
            + pltpu.roll(zc, LANES - half, 1) * c_next)


def _split_bf16(a):
    hi = a.astype(BF16)
    lo = (a - hi.astype(F32)).astype(BF16)
    return hi, lo


def _group_mean(a, avg):
    hi, lo = _split_bf16(a)
    return _dot(hi, avg) + _dot(lo, avg)


def _even_in_kernel(x_ref, g_ref, w_ref, cs_ref, cp_ref, cn_ref, avg_ref, lng_ref, lnb_ref, ws_ref,
                    bias_ref, q_ref, k_ref, v_ref, km_ref, gated_ref, *, width):
    tm = x_ref.shape[0]
    h = _rms_norm(x_ref[...], g_ref[...]).astype(BF16)
    z = _dot(h, w_ref[...])
    c_same, c_prev, c_next = cs_ref[...], cp_ref[...], cn_ref[...]
    n_lane_blocks = width // LANES
    for cb in range(n_lane_blocks):
        lo = cb * LANES
        q_ref[:, lo:lo + LANES] = (_rope(z[:, lo:lo + LANES], c_same, c_prev, c_next) * QK_SCALE).astype(BF16)
        kr = _rope(z[:, width + lo:width + lo + LANES], c_same, c_prev, c_next)
        k_ref[:, lo:lo + LANES] = kr.astype(BF16)
        for blk in range(tm // MOBA_BLOCK):
            rows = kr[blk * MOBA_BLOCK:(blk + 1) * MOBA_BLOCK]
            km_ref[blk, :, lo:lo + LANES] = jnp.mean(rows, axis=0, keepdims=True)
    v_ref[...] = z[:, 2 * width:3 * width].astype(BF16)

    gz = z[:, 3 * width:]
    gz = 0.5 * gz * (1.0 + jnp.tanh(math.sqrt(2.0 / math.pi) * (gz + 0.044715 * (gz * gz * gz))))
    u, vv = gz[:, :width], gz[:, width:]
    avg = avg_ref[...]
    cen = vv - _group_mean(vv, avg)
    var = _group_mean(cen * cen, avg)
    vn = (cen * lax.rsqrt(var + NORM_EPS) * lng_ref[...] + lnb_ref[...]).astype(BF16)

    t_idx = lax.broadcasted_iota(jnp.int32, (GMLP_CHUNK, GMLP_CHUNK), 0)
    s_idx = lax.broadcasted_iota(jnp.int32, (GMLP_CHUNK, GMLP_CHUNK), 1)
    causal = s_idx <= t_idx
    first_group = lax.broadcasted_iota(jnp.int32, (GMLP_CHUNK, LANES), 1) < GMLP_GROUP_DIM
    for pair in range(n_lane_blocks):
        lo = pair * LANES
        w0 = jnp.where(causal, ws_ref[2 * pair], 0.0).astype(BF16)
        w1 = jnp.where(causal, ws_ref[2 * pair + 1], 0.0).astype(BF16)
        bias = bias_ref[:, lo:lo + LANES]
        for c in range(tm // GMLP_CHUNK):
            r0 = c * GMLP_CHUNK
            vc = vn[r0:r0 + GMLP_CHUNK, lo:lo + LANES]
            mixed = jnp.where(first_group, _dot(w0, vc), _dot(w1, vc)) + bias
            gated_ref[r0:r0 + GMLP_CHUNK, lo:lo + LANES] = (u[r0:r0 + GMLP_CHUNK, lo:lo + LANES] * mixed).astype(BF16)


def _even_in(x, g, w_in, tables, avg, ln_g, ln_b, w_s, bias, *, seq_len, tm):
    t, d = x.shape
    width = w_in.shape[1] // 5
    n_seq_tiles = seq_len // tm
    row = pl.BlockSpec((tm, d), lambda i: (i, 0))
    tab = pl.BlockSpec((tm, LANES), lambda i: (i % n_seq_tiles, 0))
    out_row = pl.BlockSpec((tm, width), lambda i: (i, 0))
    nb = tm // MOBA_BLOCK
    act = jax.ShapeDtypeStruct((t, width), BF16)
    return pl.pallas_call(
        functools.partial(_even_in_kernel, width=width),
        grid=(t // tm,),
        in_specs=[row, _resident(), _resident(), tab, tab, tab, _resident(), _resident(), _resident(),
                  _resident(), _resident()],
        out_specs=[out_row, out_row, out_row, pl.BlockSpec((nb, 1, width), lambda i: (i, 0, 0)), out_row],
        out_shape=[act, act, act, jax.ShapeDtypeStruct((t // MOBA_BLOCK, 1, width), F32), act],
        compiler_params=_params(1),
        name="even_in",
    )(x, g, w_in, *tables, avg, ln_g, ln_b, w_s, bias)


def _stack_halves(q):
    lane = lax.broadcasted_iota(jnp.int32, q.shape, 1)
    zero = jnp.zeros_like(q)
    return jnp.concatenate([jnp.where(lane < HEAD_DIM, q, zero), jnp.where(lane >= HEAD_DIM, q, zero)], axis=0)


def _flash_scratch(rows, tk):
    scores, probs = pltpu.VMEM((rows, tk), F32), pltpu.VMEM((rows, tk), BF16)
    state = pltpu.VMEM((rows, LANES), F32)
    return [scores, scores, state, state, probs, probs, state, state, state, state, state]


def _lane_blocks(s):
    return [s[:, c * LANES:(c + 1) * LANES] for c in range(s.shape[1] // LANES)]


def _score_stage(qk_fn, t, ranges, s_ref, c_ref):
    for r0, n in ranges:
        s = qk_fn(t, r0, n)
        s_ref[r0:r0 + n, :] = s
        c_ref[r0:r0 + n, :] = functools.reduce(jnp.maximum, _lane_blocks(s))


def _softmax_stage(ranges, s_ref, c_ref, p_ref, alpha_ref, m_ref, l_ref):
    for r0, n, on_diagonal in ranges:
        rows = slice(r0, r0 + n)
        s = s_ref[rows, :]
        if on_diagonal:
            visible = lax.broadcasted_iota(jnp.int32, s.shape, 1) <= lax.broadcasted_iota(jnp.int32, s.shape, 0)
            cols = _lane_blocks(jnp.where(visible, s, NEG_INF))
            col_max = functools.reduce(jnp.maximum, cols)
        else:
            cols = _lane_blocks(s)
            col_max = c_ref[rows, :]
        m_prev = m_ref[rows, :]
        m_new = jnp.maximum(m_prev, jnp.max(col_max, axis=-1, keepdims=True))
        alpha = jnp.exp2(m_prev - m_new)
        ps = [jnp.exp2(c - m_new) for c in cols]
        l_ref[rows, :] = alpha * l_ref[rows, :] + functools.reduce(jnp.add, ps)
        p_ref[rows, :] = jnp.concatenate([p.astype(BF16) for p in ps], axis=1)
        alpha_ref[rows, :] = alpha
        m_ref[rows, :] = m_new


def _pv_stage(v, ranges, p_ref, alpha_ref, acc_ref):
    for r0, n in ranges:
        rows = slice(r0, r0 + n)
        acc_ref[rows, :] = alpha_ref[rows, :] * acc_ref[rows, :] + _dot(p_ref[rows, :], v)


def _flash_pipeline(qi, tq, qk_fn, v_fn, s0, s1, c0, c1, p0, p1, a0, a1, m_ref, l_ref, acc_ref):
    s_refs, c_refs, p_refs, a_refs = (s0, s1), (c0, c1), (p0, p1), (a0, a1)
    rows, tk = s0.shape
    tiles_per_q = tq // tk
    assert tiles_per_q % 2 == 0 and rows == 2 * tq
    n_past = qi * tiles_per_q
    whole = ((0, rows),)

    def seen_by(d):
        return tuple((half * tq + d * tk, tq - d * tk) for half in range(2))

    def softmax_ranges(d):
        out = []
        for half in range(2):
            out.append((half * tq + d * tk, tk, True))
            if (d + 1) * tk < tq:
                out.append((half * tq + (d + 1) * tk, tq - (d + 1) * tk, False))
        return tuple(out)

    def tick(t, slot, pv_ranges, sm_ranges, next_ranges):
        _pv_stage(v_fn(jnp.maximum(t - 1, 0)), pv_ranges, p_refs[1 - slot], a_refs[1 - slot], acc_ref)
        _softmax_stage(sm_ranges, s_refs[slot], c_refs[slot], p_refs[slot], a_refs[slot], m_ref, l_ref)
        if next_ranges:
            _score_stage(qk_fn, t + 1, next_ranges, s_refs[1 - slot], c_refs[1 - slot])

    m_ref[...] = jnp.full(m_ref.shape, NEG_INF, F32)
    l_ref[...] = jnp.zeros(l_ref.shape, F32)
    acc_ref[...] = jnp.zeros(acc_ref.shape, F32)
    p1[...] = jnp.zeros(p1.shape, BF16)
    a1[...] = jnp.ones(a1.shape, F32)
    _score_stage(qk_fn, 0, whole, s0, c0)

    past = ((0, rows, False),)

    def body(i, carry):
        tick(2 * i, 0, whole, past, whole)
        tick(2 * i + 1, 1, whole, past, whole)
        return carry

    lax.fori_loop(0, qi * (tiles_per_q // 2), body, 0)
    for d in range(tiles_per_q):
        tick(n_past + d, d % 2, whole if d == 0 else seen_by(d - 1), softmax_ranges(d),
             seen_by(d + 1) if d + 1 < tiles_per_q else ())
    last = tiles_per_q - 1
    _pv_stage(v_fn(n_past + last), seen_by(last), p_refs[last % 2], a_refs[last % 2], acc_ref)
    return acc_ref[...] / jnp.sum(l_ref[...], axis=-1, keepdims=True)


def _block_bias(gate, own_block, n_blocks):
    block_id = lax.broadcasted_iota(jnp.int32, gate.shape, 0)
    g = jnp.where(block_id < own_block, gate, NEG_INF)
    bias = jnp.where(block_id < n_blocks, jnp.where(block_id == own_block, 0.0, MASKED), 0.0)
    for _ in range(MOBA_TOPK):
        best = jnp.max(g, axis=0, keepdims=True)
        cand = jnp.where(g == best, jnp.where(best > NEG_INF, block_id, LANES), LANES)
        pick = block_id == jnp.min(cand, axis=0, keepdims=True)
        bias = jnp.where(pick, 0.0, bias)
        g = jnp.where(pick, NEG_INF, g)
    return bias


def _moba_kernel(q_ref, k_ref, v_ref, km_ref, o_ref, *scratch, n_blocks):
    tq = q_ref.shape[1]
    log2_block = MOBA_BLOCK.bit_length() - 1
    blocks_per_tile = tq // MOBA_BLOCK
    qi = pl.program_id(2)
    qs = _stack_halves(q_ref[0])
    n_pad = -(-n_blocks // 8) * 8
    gate = _dot_nt(km_ref[0, :n_pad, :].astype(BF16), qs)
    query = jnp.bitwise_and(lax.broadcasted_iota(jnp.int32, gate.shape, 1), tq - 1)
    own_block = qi * blocks_per_tile + jnp.right_shift(query, log2_block)
    bias = _block_bias(gate, own_block, n_blocks)
    if n_pad < LANES:
        bias = jnp.concatenate([bias, jnp.zeros((LANES - n_pad, 2 * tq), F32)], axis=0)
    bias = bias.T
    q_aug = jnp.concatenate([qs, bias.astype(BF16)], axis=1)

    tk = scratch[0].shape[1]
    key_lane = lax.broadcasted_iota(jnp.int32, (tk, LANES), 1)
    key_block = jnp.right_shift(lax.broadcasted_iota(jnp.int32, (tk, LANES), 0), log2_block)

    def scores(t, r0, n):
        one_hot = jnp.where(key_lane == t * (tk // MOBA_BLOCK) + key_block, 1.0, 0.0).astype(BF16)
        k = k_ref[0, pl.ds(pl.multiple_of(t * tk, tk), tk), :]
        return _dot_nt(q_aug[r0:r0 + n], jnp.concatenate([k, one_hot], axis=1))

    def values(t):
        return v_ref[0, pl.ds(pl.multiple_of(t * tk, tk), tk), :]

    o = _flash_pipeline(qi, tq, scores, values, *scratch)
    out_lane = lax.broadcasted_iota(jnp.int32, (tq, LANES), 1)
    o_ref[0] = jnp.where(out_lane < HEAD_DIM, o[:tq], o[tq:]).astype(BF16)


def _moba(q, k, v, km, *, tq, tk):
    b, s, width = q.shape
    n_blocks = s // MOBA_BLOCK
    assert n_blocks <= LANES and tk % MOBA_BLOCK == 0 and tq & (tq - 1) == 0
    qspec = pl.BlockSpec((1, tq, LANES), lambda bi, hp, qi: (bi, qi, hp))
    kvspec = pl.BlockSpec((1, s, LANES), lambda bi, hp, qi: (bi, 0, hp))
    return pl.pallas_call(
        functools.partial(_moba_kernel, n_blocks=n_blocks),
        grid=(b, width // LANES, s // tq),
        in_specs=[qspec, kvspec, kvspec, pl.BlockSpec((1, LANES, LANES), lambda bi, hp, qi: (bi, 0, hp))],
        out_specs=qspec,
        out_shape=jax.ShapeDtypeStruct((b, s, width), BF16),
        scratch_shapes=_flash_scratch(2 * tq, tk),
        compiler_params=_params(3),
        name="moba_attn",
    )(q, k, v, km)


def _odd_in_kernel(x_ref, g_ref, w_ref, cs_ref, cp_ref, cn_ref, q_ref, k_ref, v_ref, *, qk_width):
    h = _rms_norm(x_ref[...], g_ref[...]).astype(BF16)
    z = _dot(h, w_ref[...])
    c_same, c_prev, c_next = cs_ref[...], cp_ref[...], cn_ref[...]
    for cb in range(qk_width // LANES):
        lo = cb * LANES
        q_ref[:, lo:lo + LANES] = (_rope(z[:, lo:lo + LANES], c_same, c_prev, c_next) * QK_SCALE).astype(BF16)
        k_ref[:, lo:lo + LANES] = _rope(z[:, qk_width + lo:qk_width + lo + LANES], c_same, c_prev, c_next).astype(BF16)
    v_ref[...] = z[:, 2 * qk_width:].astype(BF16)


def _odd_in(x, g, w_in, tables, *, qk_width, seq_len, tm):
    t, d = x.shape
    v_width = w_in.shape[1] - 2 * qk_width
    n_seq_tiles = seq_len // tm
    row = pl.BlockSpec((tm, d), lambda i: (i, 0))
    tab = pl.BlockSpec((tm, LANES), lambda i: (i % n_seq_tiles, 0))
    qk_row = pl.BlockSpec((tm, qk_width), lambda i: (i, 0))
    qk = jax.ShapeDtypeStruct((t, qk_width), BF16)
    return pl.pallas_call(
        functools.partial(_odd_in_kernel, qk_width=qk_width),
        grid=(t // tm,),
        in_specs=[row, _resident(), _resident(), tab, tab, tab],
        out_specs=[qk_row, qk_row, pl.BlockSpec((tm, v_width), lambda i: (i, 0))],
        out_shape=[qk, qk, jax.ShapeDtypeStruct((t, v_width), BF16)],
        compiler_params=_params(1),
        name="odd_in",
    )(x, g, w_in, *tables)


def _diff_kernel(lq1_ref, lk1_ref, lq2_ref, lk2_ref, g_ref, q_ref, k_ref, v_ref, o_ref, *scratch, lambda_init):
    tq = q_ref.shape[1]
    qs = _stack_halves(q_ref[0])

    tk = scratch[0].shape[1]

    def scores(t, r0, n):
        return _dot_nt(qs[r0:r0 + n], k_ref[0, pl.ds(pl.multiple_of(t * tk, tk), tk), :])

    def values(t):
        return v_ref[0, pl.ds(pl.multiple_of(t * tk, tk), tk), :]

    o = _flash_pipeline(pl.program_id(2), tq, scores, values, *scratch)
    lam = (jnp.exp(jnp.sum(lq1_ref[...] * lk1_ref[...], axis=-1, keepdims=True))
           - jnp.exp(jnp.sum(lq2_ref[...] * lk2_ref[...], axis=-1, keepdims=True)) + lambda_init)
    o = o[:tq] - lam * o[tq:]
    o_ref[0] = (_rms_norm(o, g_ref[...]) * (1.0 - lambda_init)).astype(BF16)


def _diff_attn(q, k, v, lq1, lk1, lq2, lk2, subln_g, *, lambda_init, tq, tk):
    b, s, width = q.shape
    qspec = pl.BlockSpec((1, tq, LANES), lambda bi, h, qi: (bi, qi, h))
    kvspec = pl.BlockSpec((1, s, LANES), lambda bi, h, qi: (bi, 0, h))
    return pl.pallas_call(
        functools.partial(_diff_kernel, lambda_init=lambda_init),
        grid=(b, width // LANES, s // tq),
        in_specs=[_resident()] * 5 + [qspec, kvspec, kvspec],
        out_specs=qspec,
        out_shape=jax.ShapeDtypeStruct((b, s, width), BF16),
        scratch_shapes=_flash_scratch(2 * tq, tk),
        compiler_params=_params(3),
        name="diff_attn",
    )(lq1, lk1, lq2, lk2, subln_g, q, k, v)


def kernel(x, ffn_pre_norm, ffn_pre_w_gate, ffn_pre_w_up, ffn_pre_w_down, mix_norm, ffn_post_norm, ffn_post_w_gate, ffn_post_w_up, ffn_post_w_down, even_w_in, even_w_out, gmlp_ln_g, gmlp_ln_b, gmlp_w_s, gmlp_b_s, odd_w_in, odd_w_out, diff_lambda_q1, diff_lambda_k1, diff_lambda_q2, diff_lambda_k2, diff_subln_g, final_norm):
    b, s, d = x.shape
    depth = ffn_pre_norm.shape[0]
    t = b * s
    tm = 512
    tables = _rope_tables(s)
    xf = x.reshape(t, d)
    fin = final_norm.reshape(1, d)

    def ffn(xf, mix, mix_w, norm, wg, wu, wd, final):
        return _ffn(xf, mix, mix_w, norm.reshape(1, d), wg.astype(BF16), wu.astype(BF16), wd.astype(BF16), fin,
                    final_norm=final, tm=tm)

    for layer in range(depth):
        xf = ffn(xf, [], [], ffn_pre_norm[layer], ffn_pre_w_gate[layer], ffn_pre_w_up[layer],
                 ffn_pre_w_down[layer], False)
        g_mix = mix_norm[layer].reshape(1, d)
        if layer % 2 == 0:
            e = layer // 2
            width = GMLP_GROUPS * GMLP_GROUP_DIM
            n_blocks = s // MOBA_BLOCK
            group = jnp.arange(width) // GMLP_GROUP_DIM
            avg = (group[:, None] == group[None, :]).astype(BF16) / GMLP_GROUP_DIM
            bias = jnp.repeat(gmlp_b_s[e].T, GMLP_GROUP_DIM, axis=1)
            q, k, v, km, gated = _even_in(
                xf, g_mix, even_w_in[e].astype(BF16), tables, avg, gmlp_ln_g[e].reshape(1, width),
                gmlp_ln_b[e].reshape(1, width), gmlp_w_s[e], bias, seq_len=s, tm=tm)
            km = jnp.pad(km.reshape(b, n_blocks, width), ((0, 0), (0, LANES - n_blocks), (0, 0)))
            attn = _moba(q.reshape(b, s, width), k.reshape(b, s, width), v.reshape(b, s, width), km,
                         tq=min(ATTN_Q_TILE, s), tk=ATTN_KV_TILE)
            w_out = even_w_out[e].astype(BF16)
            mix, mix_w = [attn.reshape(t, width), gated], [w_out[:width], w_out[width:]]
        else:
            o = layer // 2
            lambda_init = 0.8 - 0.6 * math.exp(-0.3 * layer)
            v_width = odd_w_out.shape[1]
            qk_width = (odd_w_in.shape[2] - v_width) // 2
            q, k, v = _odd_in(xf, g_mix, odd_w_in[o].astype(BF16), tables, qk_width=qk_width, seq_len=s, tm=tm)
            vec = lambda a: a[o].reshape(1, -1)
            attn = _diff_attn(q.reshape(b, s, qk_width), k.reshape(b, s, qk_width), v.reshape(b, s, v_width),
                              vec(diff_lambda_q1), vec(diff_lambda_k1), vec(diff_lambda_q2), vec(diff_lambda_k2),
                              vec(diff_subln_g), lambda_init=lambda_init, tq=min(ATTN_Q_TILE, s),
                              tk=ATTN_KV_TILE)
            mix, mix_w = [attn.reshape(t, v_width)], [odd_w_out[o].astype(BF16)]
        xf = ffn(xf, mix, mix_w, ffn_post_norm[layer], ffn_post_w_gate[layer], ffn_post_w_up[layer],
                 ffn_post_w_down[layer], layer == depth - 1)
    return xf.reshape(b, s, d)
```

```python
import functools
import math

import jax
import jax.numpy as jnp
from jax import lax
from jax.experimental import pallas as pl
from jax.experimental.pallas import tpu as pltpu

F32 = jnp.float32
BF16 = jnp.bfloat16

HEAD_DIM = 64
ROT_DIM = HEAD_DIM // 4
ROPE_THETA = 500000.0
NORM_EPS = 1e-6
MOBA_BLOCK = 256
MOBA_TOPK = 3
GMLP_GROUPS = 8
GMLP_GROUP_DIM = 64
GMLP_CHUNK = 128

LANES = 128
VMEM_LIMIT_BYTES = 56 * 1024 * 1024
ATTN_Q_TILE = 1024
ATTN_KV_TILE = 512

NEG_INF = float("-inf")
MASKED = -1e30
QK_SCALE = HEAD_DIM ** -0.5 * math.log2(math.e)


def _params(n_grid_axes):
    return pltpu.CompilerParams(
        dimension_semantics=("parallel",) * (n_grid_axes - 1) + ("arbitrary",),
        vmem_limit_bytes=VMEM_LIMIT_BYTES,
    )


def _resident():
    return pl.BlockSpec(memory_space=pltpu.VMEM)


def _rms_norm(x, g):
    ms = jnp.mean(x * x, axis=-1, keepdims=True)
    return x * lax.rsqrt(ms + NORM_EPS) * g


def _dot(a, b):
    return jnp.dot(a, b, preferred_element_type=F32)


def _dot_nt(a, b):
    return lax.dot_general(a, b, (((1,), (1,)), ((), ())), preferred_element_type=F32)


def _ffn_kernel(*refs, n_mix, final_norm):
    x_ref, mix_refs, mix_w_refs = refs[0], refs[1:1 + n_mix], refs[1 + n_mix:1 + 2 * n_mix]
    g_ref, wg_ref, wu_ref, wd_ref, fin_ref, o_ref = refs[1 + 2 * n_mix:]
    x = x_ref[...]
    for a_ref, w_ref in zip(mix_refs, mix_w_refs):
        x = x + _dot(a_ref[...], w_ref[...])
    h = _rms_norm(x, g_ref[...]).astype(BF16)
    gate = _dot(h, wg_ref[...])
    up = _dot(h, wu_ref[...])
    act = (gate * jax.nn.sigmoid(gate) * up).astype(BF16)
    y = x + 0.5 * _dot(act, wd_ref[...])
    if final_norm:
        y = _rms_norm(y, fin_ref[...])
    o_ref[...] = y


def _ffn(x, mix, mix_w, g, wg, wu, wd, fin, *, final_norm, tm):
    t, d = x.shape
    row = pl.BlockSpec((tm, d), lambda i: (i, 0))
    mix_specs = [pl.BlockSpec((tm, a.shape[1]), lambda i: (i, 0)) for a in mix]
    return pl.pallas_call(
        functools.partial(_ffn_kernel, n_mix=len(mix), final_norm=final_norm),
        grid=(t // tm,),
        in_specs=[row] + mix_specs + [_resident()] * (len(mix_w) + 5),
        out_specs=row,
        out_shape=jax.ShapeDtypeStruct((t, d), F32),
        compiler_params=_params(1),
        name="ffn_final" if final_norm else ("ffn_mix" if mix else "ffn"),
    )(x, *mix, *mix_w, g, wg, wu, wd, fin)


def _rope_tables(seq_len):
    half = ROT_DIM // 2
    inv = 1.0 / (ROPE_THETA ** (jnp.arange(0, ROT_DIM, 2, dtype=F32) / ROT_DIM))
    ang = jnp.arange(seq_len, dtype=F32)[:, None] * inv[None, :]
    cos, sin = jnp.cos(ang), jnp.sin(ang)
    ones = jnp.ones((seq_len, HEAD_DIM - ROT_DIM), F32)
    zeros_h = jnp.zeros((seq_len, half), F32)
    zeros_r = jnp.zeros((seq_len, HEAD_DIM - ROT_DIM), F32)
    c_same = jnp.concatenate([cos, cos, ones], axis=-1)
    c_prev = jnp.concatenate([zeros_h, sin, zeros_r], axis=-1)
    c_next = jnp.concatenate([-sin, zeros_h, zeros_r], axis=-1)
    rep = LANES // HEAD_DIM
    return tuple(jnp.tile(c, (1, rep)) for c in (c_same, c_prev, c_next))


def _rope(zc, c_same, c_prev, c_next):
    half = ROT_DIM // 2
    return (zc * c_same + pltpu.roll(zc, half, 1) * c_prev
            + pltpu.roll(zc, LANES - half, 1) * c_next)


def _group_mean(a, avg):
    slab = avg.shape[0]
    return jnp.concatenate([_dot(a[:, lo:lo + slab].astype(BF16), avg) for lo in range(0, a.shape[1], slab)], axis=1)


def _even_in_kernel(x_ref, g_ref, w_ref, cs_ref, cp_ref, cn_ref, avg_ref, lng_ref, lnb_ref, ws_ref,
                    bias_ref, q_ref, k_ref, v_ref, km_ref, gated_ref, *, width):
    tm = x_ref.shape[0]
    h = _rms_norm(x_ref[...], g_ref[...]).astype(BF16)
    z = _dot(h, w_ref[...])
    c_same, c_prev, c_next = cs_ref[...], cp_ref[...], cn_ref[...]
    n_lane_blocks = width // LANES
    for cb in range(n_lane_blocks):
        lo = cb * LANES
        q_ref[:, lo:lo + LANES] = (_rope(z[:, lo:lo + LANES], c_same, c_prev, c_next) * QK_SCALE).astype(BF16)
        kr = _rope(z[:, width + lo:width + lo + LANES], c_same, c_prev, c_next)
        k_ref[:, lo:lo + LANES] = kr.astype(BF16)
        for blk in range(tm // MOBA_BLOCK):
            rows = kr[blk * MOBA_BLOCK:(blk + 1) * MOBA_BLOCK]
            km_ref[blk, :, lo:lo + LANES] = jnp.mean(rows, axis=0, keepdims=True)
    v_ref[...] = z[:, 2 * width:3 * width].astype(BF16)

    gz = z[:, 3 * width:]
    gz = 0.5 * gz * (1.0 + jnp.tanh(math.sqrt(2.0 / math.pi) * (gz + 0.044715 * (gz * gz * gz))))
    u, vv = gz[:, :width], gz[:, width:]
    avg = avg_ref[...]
    cen = vv - _group_mean(vv, avg)
    var = _group_mean(cen * cen, avg)
    vn = (cen * lax.rsqrt(var + NORM_EPS) * lng_ref[...] + lnb_ref[...]).astype(BF16)

    t_idx = lax.broadcasted_iota(jnp.int32, (GMLP_CHUNK, GMLP_CHUNK), 0)
    s_idx = lax.broadcasted_iota(jnp.int32, (GMLP_CHUNK, GMLP_CHUNK), 1)
    causal = s_idx <= t_idx
    first_group = lax.broadcasted_iota(jnp.int32, (GMLP_CHUNK, LANES), 1) < GMLP_GROUP_DIM
    chunks = range(0, tm, GMLP_CHUNK)
    for pair in range(n_lane_blocks):
        lo = pair * LANES
        w0 = jnp.where(causal, ws_ref[2 * pair], 0.0).astype(BF16)
        w1 = jnp.where(causal, ws_ref[2 * pair + 1], 0.0).astype(BF16)
        bias = bias_ref[:, lo:lo + LANES]
        vc = jnp.concatenate([vn[r0:r0 + GMLP_CHUNK, lo:lo + LANES] for r0 in chunks], axis=1)
        m0, m1 = _dot(w0, vc), _dot(w1, vc)
        for c, r0 in enumerate(chunks):
            mixed = jnp.where(first_group, m0[:, c * LANES:(c + 1) * LANES], m1[:, c * LANES:(c + 1) * LANES]) + bias
            gated_ref[r0:r0 + GMLP_CHUNK, lo:lo + LANES] = (u[r0:r0 + GMLP_CHUNK, lo:lo + LANES] * mixed).astype(BF16)


def _even_in(x, g, w_in, tables, avg, ln_g, ln_b, w_s, bias, *, seq_len, tm):
    t, d = x.shape
    width = w_in.shape[1] // 5
    n_seq_tiles = seq_len // tm
    row = pl.BlockSpec((tm, d), lambda i: (i, 0))
    tab = pl.BlockSpec((tm, LANES), lambda i: (i % n_seq_tiles, 0))
    out_row = pl.BlockSpec((tm, width), lambda i: (i, 0))
    nb = tm // MOBA_BLOCK
    act = jax.ShapeDtypeStruct((t, width), BF16)
    return pl.pallas_call(
        functools.partial(_even_in_kernel, width=width),
        grid=(t // tm,),
        in_specs=[row, _resident(), _resident(), tab, tab, tab, _resident(), _resident(), _resident(),
                  _resident(), _resident()],
        out_specs=[out_row, out_row, out_row, pl.BlockSpec((nb, 1, width), lambda i: (i, 0, 0)), out_row],
        out_shape=[act, act, act, jax.ShapeDtypeStruct((t // MOBA_BLOCK, 1, width), F32), act],
        compiler_params=_params(1),
        name="even_in",
    )(x, g, w_in, *tables, avg, ln_g, ln_b, w_s, bias)


def _stack_halves(q):
    lane = lax.broadcasted_iota(jnp.int32, q.shape, 1)
    zero = jnp.zeros_like(q)
    return jnp.concatenate([jnp.where(lane < HEAD_DIM, q, zero), jnp.where(lane >= HEAD_DIM, q, zero)], axis=0)


def _flash_scratch(rows, tk):
    scores, probs = pltpu.VMEM((rows, tk), F32), pltpu.VMEM((rows, tk), BF16)
    state = pltpu.VMEM((rows, LANES), F32)
    return [scores, scores, state, state, probs, probs, state, state, state, state, state]


def _lane_blocks(s):
    return [s[:, c * LANES:(c + 1) * LANES] for c in range(s.shape[1] // LANES)]


def _score_stage(qk_fn, t, ranges, s_ref, c_ref):
    for r0, n in ranges:
        s = qk_fn(t, r0, n)
        s_ref[r0:r0 + n, :] = s
        c_ref[r0:r0 + n, :] = functools.reduce(jnp.maximum, _lane_blocks(s))


def _softmax_stage(ranges, s_ref, c_ref, p_ref, alpha_ref, m_ref, l_ref, *, start):
    for r0, n, on_diagonal in ranges:
        rows = slice(r0, r0 + n)
        s = s_ref[rows, :]
        if on_diagonal:
            visible = lax.broadcasted_iota(jnp.int32, s.shape, 1) <= lax.broadcasted_iota(jnp.int32, s.shape, 0)
            cols = _lane_blocks(jnp.where(visible, s, NEG_INF))
            col_max = functools.reduce(jnp.maximum, cols)
        else:
            cols = _lane_blocks(s)
            col_max = c_ref[rows, :]
        m_new = jnp.max(col_max, axis=-1, keepdims=True)
        if start:
            m_new = jnp.broadcast_to(m_new, col_max.shape)
        else:
            m_prev = m_ref[rows, :]
            m_new = jnp.maximum(m_prev, m_new)
            alpha = jnp.exp2(m_prev - m_new)
            alpha_ref[rows, :] = alpha
        ps = [jnp.exp2(c - m_new) for c in cols]
        l_new = functools.reduce(jnp.add, ps)
        l_ref[rows, :] = l_new if start else alpha * l_ref[rows, :] + l_new
        p_ref[rows, :] = jnp.concatenate([p.astype(BF16) for p in ps], axis=1)
        m_ref[rows, :] = m_new


def _pv_stage(v, ranges, p_ref, alpha_ref, acc_ref, *, start):
    for r0, n in ranges:
        rows = slice(r0, r0 + n)
        pv = _dot(p_ref[rows, :], v)
        acc_ref[rows, :] = pv if start else alpha_ref[rows, :] * acc_ref[rows, :] + pv


def _flash_pipeline(qi, tq, qk_fn, v_fn, s0, s1, c0, c1, p0, p1, a0, a1, m_ref, l_ref, acc_ref):
    s_refs, c_refs, p_refs, a_refs = (s0, s1), (c0, c1), (p0, p1), (a0, a1)
    rows, tk = s0.shape
    tiles_per_q = tq // tk
    assert tiles_per_q % 2 == 0 and rows == 2 * tq
    n_past = qi * tiles_per_q
    whole = ((0, rows),)

    def seen_by(d):
        return tuple((half * tq + d * tk, tq - d * tk) for half in range(2))

    def softmax_ranges(d):
        out = []
        for half in range(2):
            out.append((half * tq + d * tk, tk, True))
            if (d + 1) * tk < tq:
                out.append((half * tq + (d + 1) * tk, tq - (d + 1) * tk, False))
        return tuple(out)

    def tick(t, slot, pv_ranges, sm_ranges, next_ranges, *, age):
        if age > 0:
            _pv_stage(v_fn(t - 1), pv_ranges, p_refs[1 - slot], a_refs[1 - slot], acc_ref, start=age == 1)
        _softmax_stage(sm_ranges, s_refs[slot], c_refs[slot], p_refs[slot], a_refs[slot], m_ref, l_ref,
                       start=age == 0)
        if next_ranges:
            _score_stage(qk_fn, t + 1, next_ranges, s_refs[1 - slot], c_refs[1 - slot])

    past = ((0, rows, False),)

    def past_pair(i, age):
        tick(2 * i, 0, whole, past, whole, age=age)
        tick(2 * i + 1, 1, whole, past, whole, age=min(age + 1, 2))

    def diagonal_ticks(age):
        for d in range(tiles_per_q):
            tick(n_past + d, d % 2, whole if d == 0 else seen_by(d - 1), softmax_ranges(d),
                 seen_by(d + 1) if d + 1 < tiles_per_q else (), age=min(age + d, 2))

    _score_stage(qk_fn, 0, whole, s0, c0)

    @pl.when(qi == 0)
    def _():
        diagonal_ticks(0)

    @pl.when(qi > 0)
    def _():
        past_pair(0, 0)

        def body(i, carry):
            past_pair(i, 2)
            return carry

        lax.fori_loop(1, qi * (tiles_per_q // 2), body, 0)
        diagonal_ticks(2)

    last = tiles_per_q - 1
    _pv_stage(v_fn(n_past + last), seen_by(last), p_refs[last % 2], a_refs[last % 2], acc_ref, start=False)
    return acc_ref[...] / jnp.sum(l_ref[...], axis=-1, keepdims=True)


def _block_bias(gate, own_block, n_blocks):
    block_id = lax.broadcasted_iota(jnp.int32, gate.shape, 0)
    g = jnp.where(block_id < own_block, gate, NEG_INF)
    bias = jnp.where(block_id < n_blocks, jnp.where(block_id == own_block, 0.0, MASKED), 0.0)
    for _ in range(MOBA_TOPK):
        best = jnp.max(g, axis=0, keepdims=True)
        cand = jnp.where(g == best, jnp.where(best > NEG_INF, block_id, LANES), LANES)
        pick = block_id == jnp.min(cand, axis=0, keepdims=True)
        bias = jnp.where(pick, 0.0, bias)
        g = jnp.where(pick, NEG_INF, g)
    return bias


def _moba_kernel(q_ref, k_ref, v_ref, km_ref, o_ref, *scratch, n_blocks):
    tq = q_ref.shape[1]
    log2_block = MOBA_BLOCK.bit_length() - 1
    blocks_per_tile = tq // MOBA_BLOCK
    qi = pl.program_id(2)
    qs = _stack_halves(q_ref[0])
    n_pad = -(-n_blocks // 8) * 8
    gate = _dot_nt(km_ref[0, :n_pad, :].astype(BF16), qs)
    query = jnp.bitwise_and(lax.broadcasted_iota(jnp.int32, gate.shape, 1), tq - 1)
    own_block = qi * blocks_per_tile + jnp.right_shift(query, log2_block)
    bias = _block_bias(gate, own_block, n_blocks)
    if n_pad < LANES:
        bias = jnp.concatenate([bias, jnp.zeros((LANES - n_pad, 2 * tq), F32)], axis=0)
    bias = bias.T
    q_aug = jnp.concatenate([qs, bias.astype(BF16)], axis=1)

    tk = scratch[0].shape[1]
    key_lane = lax.broadcasted_iota(jnp.int32, (tk, LANES), 1)
    key_block = jnp.right_shift(lax.broadcasted_iota(jnp.int32, (tk, LANES), 0), log2_block)

    def scores(t, r0, n):
        one_hot = jnp.where(key_lane == t * (tk // MOBA_BLOCK) + key_block, 1.0, 0.0).astype(BF16)
        k = k_ref[0, pl.ds(pl.multiple_of(t * tk, tk), tk), :]
        return _dot_nt(q_aug[r0:r0 + n], jnp.concatenate([k, one_hot], axis=1))

    def values(t):
        return v_ref[0, pl.ds(pl.multiple_of(t * tk, tk), tk), :]

    o = _flash_pipeline(qi, tq, scores, values, *scratch)
    out_lane = lax.broadcasted_iota(jnp.int32, (tq, LANES), 1)
    o_ref[0] = jnp.where(out_lane < HEAD_DIM, o[:tq], o[tq:]).astype(BF16)


def _moba(q, k, v, km, *, tq, tk):
    b, s, width = q.shape
    n_blocks = s // MOBA_BLOCK
    assert n_blocks <= LANES and tk % MOBA_BLOCK == 0 and tq & (tq - 1) == 0
    qspec = pl.BlockSpec((1, tq, LANES), lambda bi, hp, qi: (bi, qi, hp))
    kvspec = pl.BlockSpec((1, s, LANES), lambda bi, hp, qi: (bi, 0, hp))
    return pl.pallas_call(
        functools.partial(_moba_kernel, n_blocks=n_blocks),
        grid=(b, width // LANES, s // tq),
        in_specs=[qspec, kvspec, kvspec, pl.BlockSpec((1, LANES, LANES), lambda bi, hp, qi: (bi, 0, hp))],
        out_specs=qspec,
        out_shape=jax.ShapeDtypeStruct((b, s, width), BF16),
        scratch_shapes=_flash_scratch(2 * tq, tk),
        compiler_params=_params(3),
        name="moba_attn",
    )(q, k, v, km)


def _odd_in_kernel(x_ref, g_ref, w_ref, cs_ref, cp_ref, cn_ref, q_ref, k_ref, v_ref, *, qk_width):
    h = _rms_norm(x_ref[...], g_ref[...]).astype(BF16)
    z = _dot(h, w_ref[...])
    c_same, c_prev, c_next = cs_ref[...], cp_ref[...], cn_ref[...]
    for cb in range(qk_width // LANES):
        lo = cb * LANES
        q_ref[:, lo:lo + LANES] = (_rope(z[:, lo:lo + LANES], c_same, c_prev, c_next) * QK_SCALE).astype(BF16)
        k_ref[:, lo:lo + LANES] = _rope(z[:, qk_width + lo:qk_width + lo + LANES], c_same, c_prev, c_next).astype(BF16)
    v_ref[...] = z[:, 2 * qk_width:].astype(BF16)


def _odd_in(x, g, w_in, tables, *, qk_width, seq_len, tm):
    t, d = x.shape
    v_width = w_in.shape[1] - 2 * qk_width
    n_seq_tiles = seq_len // tm
    row = pl.BlockSpec((tm, d), lambda i: (i, 0))
    tab = pl.BlockSpec((tm, LANES), lambda i: (i % n_seq_tiles, 0))
    qk_row = pl.BlockSpec((tm, qk_width), lambda i: (i, 0))
    qk = jax.ShapeDtypeStruct((t, qk_width), BF16)
    return pl.pallas_call(
        functools.partial(_odd_in_kernel, qk_width=qk_width),
        grid=(t // tm,),
        in_specs=[row, _resident(), _resident(), tab, tab, tab],
        out_specs=[qk_row, qk_row, pl.BlockSpec((tm, v_width), lambda i: (i, 0))],
        out_shape=[qk, qk, jax.ShapeDtypeStruct((t, v_width), BF16)],
        compiler_params=_params(1),
        name="odd_in",
    )(x, g, w_in, *tables)


def _diff_kernel(lq1_ref, lk1_ref, lq2_ref, lk2_ref, g_ref, q_ref, k_ref, v_ref, o_ref, *scratch, lambda_init):
    tq = q_ref.shape[1]
    qs = _stack_halves(q_ref[0])

    tk = scratch[0].shape[1]

    def scores(t, r0, n):
        return _dot_nt(qs[r0:r0 + n], k_ref[0, pl.ds(pl.multiple_of(t * tk, tk), tk), :])

    def values(t):
        return v_ref[0, pl.ds(pl.multiple_of(t * tk, tk), tk), :]

    o = _flash_pipeline(pl.program_id(2), tq, scores, values, *scratch)
    lam = (jnp.exp(jnp.sum(lq1_ref[...] * lk1_ref[...], axis=-1, keepdims=True))
           - jnp.exp(jnp.sum(lq2_ref[...] * lk2_ref[...], axis=-1, keepdims=True)) + lambda_init)
    o = o[:tq] - lam * o[tq:]
    o_ref[0] = (_rms_norm(o, g_ref[...]) * (1.0 - lambda_init)).astype(BF16)


def _diff_attn(q, k, v, lq1, lk1, lq2, lk2, subln_g, *, lambda_init, tq, tk):
    b, s, width = q.shape
    qspec = pl.BlockSpec((1, tq, LANES), lambda bi, h, qi: (bi, qi, h))
    kvspec = pl.BlockSpec((1, s, LANES), lambda bi, h, qi: (bi, 0, h))
    return pl.pallas_call(
        functools.partial(_diff_kernel, lambda_init=lambda_init),
        grid=(b, width // LANES, s // tq),
        in_specs=[_resident()] * 5 + [qspec, kvspec, kvspec],
        out_specs=qspec,
        out_shape=jax.ShapeDtypeStruct((b, s, width), BF16),
        scratch_shapes=_flash_scratch(2 * tq, tk),
        compiler_params=_params(3),
        name="diff_attn",
    )(lq1, lk1, lq2, lk2, subln_g, q, k, v)


def kernel(x, ffn_pre_norm, ffn_pre_w_gate, ffn_pre_w_up, ffn_pre_w_down, mix_norm, ffn_post_norm, ffn_post_w_gate, ffn_post_w_up, ffn_post_w_down, even_w_in, even_w_out, gmlp_ln_g, gmlp_ln_b, gmlp_w_s, gmlp_b_s, odd_w_in, odd_w_out, diff_lambda_q1, diff_lambda_k1, diff_lambda_q2, diff_lambda_k2, diff_subln_g, final_norm):
    b, s, d = x.shape
    depth = ffn_pre_norm.shape[0]
    t = b * s
    tm = 512
    tables = _rope_tables(s)
    xf = x.reshape(t, d)
    fin = final_norm.reshape(1, d)

    def ffn(xf, mix, mix_w, norm, wg, wu, wd, final):
        return _ffn(xf, mix, mix_w, norm.reshape(1, d), wg.astype(BF16), wu.astype(BF16), wd.astype(BF16), fin,
                    final_norm=final, tm=tm)

    for layer in range(depth):
        xf = ffn(xf, [], [], ffn_pre_norm[layer], ffn_pre_w_gate[layer], ffn_pre_w_up[layer],
                 ffn_pre_w_down[layer], False)
        g_mix = mix_norm[layer].reshape(1, d)
        if layer % 2 == 0:
            e = layer // 2
            width = GMLP_GROUPS * GMLP_GROUP_DIM
            n_blocks = s // MOBA_BLOCK
            group = jnp.arange(2 * LANES) // GMLP_GROUP_DIM
            avg = (group[:, None] == group[None, :]).astype(BF16) / GMLP_GROUP_DIM
            bias = jnp.repeat(gmlp_b_s[e].T, GMLP_GROUP_DIM, axis=1)
            q, k, v, km, gated = _even_in(
                xf, g_mix, even_w_in[e].astype(BF16), tables, avg, gmlp_ln_g[e].reshape(1, width),
                gmlp_ln_b[e].reshape(1, width), gmlp_w_s[e], bias, seq_len=s, tm=tm)
            km = jnp.pad(km.reshape(b, n_blocks, width), ((0, 0), (0, LANES - n_blocks), (0, 0)))
            attn = _moba(q.reshape(b, s, width), k.reshape(b, s, width), v.reshape(b, s, width), km,
                         tq=min(ATTN_Q_TILE, s), tk=ATTN_KV_TILE)
            w_out = even_w_out[e].astype(BF16)
            mix, mix_w = [attn.reshape(t, width), gated], [w_out[:width], w_out[width:]]
        else:
            o = layer // 2
            lambda_init = 0.8 - 0.6 * math.exp(-0.3 * layer)
            v_width = odd_w_out.shape[1]
            qk_width = (odd_w_in.shape[2] - v_width) // 2
            q, k, v = _odd_in(xf, g_mix, odd_w_in[o].astype(BF16), tables, qk_width=qk_width, seq_len=s, tm=tm)
            vec = lambda a: a[o].reshape(1, -1)
            attn = _diff_attn(q.reshape(b, s, qk_width), k.reshape(b, s, qk_width), v.reshape(b, s, v_width),
                              vec(diff_lambda_q1), vec(diff_lambda_k1), vec(diff_lambda_q2), vec(diff_lambda_k2),
                              vec(diff_subln_g), lambda_init=lambda_init, tq=min(ATTN_Q_TILE, s),
                              tk=ATTN_KV_TILE)
            mix, mix_w = [attn.reshape(t, v_width)], [odd_w_out[o].astype(BF16)]
        xf = ffn(xf, mix, mix_w, ffn_post_norm[layer], ffn_post_w_gate[layer], ffn_post_w_up[layer],
                 ffn_post_w_down[layer], layer == depth - 1)
    return xf.reshape(b, s, d)
```

```python
import functools
import math

import jax
import jax.numpy as jnp
from jax import lax
from jax.experimental import pallas as pl
from jax.experimental.pallas import tpu as pltpu

F32 = jnp.float32
BF16 = jnp.bfloat16

HEAD_DIM = 64
ROT_DIM = HEAD_DIM // 4
ROPE_THETA = 500000.0
NORM_EPS = 1e-6
MOBA_BLOCK = 256
MOBA_TOPK = 3
GMLP_GROUPS = 8
GMLP_GROUP_DIM = 64
GMLP_CHUNK = 128

LANES = 128
VMEM_LIMIT_BYTES = 56 * 1024 * 1024
ATTN_Q_TILE = 1024
ATTN_KV_TILE = 512

NEG_INF = float("-inf")
MASKED = -1e30
QK_SCALE = HEAD_DIM ** -0.5 * math.log2(math.e)


def _params(n_grid_axes):
    return pltpu.CompilerParams(
        dimension_semantics=("parallel",) * (n_grid_axes - 1) + ("arbitrary",),
        vmem_limit_bytes=VMEM_LIMIT_BYTES,
    )


def _resident():
    return pl.BlockSpec(memory_space=pltpu.VMEM)


def _rms_norm(x, g):
    ms = jnp.mean(x * x, axis=-1, keepdims=True)
    return x * lax.rsqrt(ms + NORM_EPS) * g


def _dot(a, b):
    return jnp.dot(a, b, preferred_element_type=F32)


def _dot_nt(a, b):
    return lax.dot_general(a, b, (((1,), (1,)), ((), ())), preferred_element_type=F32)


def _cast_kernel(x_ref, o_ref):
    o_ref[...] = x_ref[...].astype(o_ref.dtype)


def _layer_bf16(w, layer):
    _, rows, cols = w.shape
    budget = 3 * 1024 * 1024
    rb = max(r for r in range(16, rows + 1, 16) if rows % r == 0 and r * cols * 4 <= budget)
    return pl.pallas_call(
        _cast_kernel,
        grid=(rows // rb,),
        in_specs=[pl.BlockSpec((None, rb, cols), lambda i: (layer, i, 0))],
        out_specs=pl.BlockSpec((rb, cols), lambda i: (i, 0)),
        out_shape=jax.ShapeDtypeStruct((rows, cols), BF16),
        compiler_params=_params(1),
        name="to_bf16",
    )(w)


def _ffn_kernel(*refs, n_mix, final_norm):
    x_ref, mix_refs = refs[0], refs[1:1 + n_mix]
    mix_w_ref = refs[1 + n_mix] if n_mix else None
    g_ref, wg_ref, wu_ref, wd_ref, fin_ref, o_ref = refs[1 + n_mix + (1 if n_mix else 0):]
    x = x_ref[...]
    row = 0
    for a_ref in mix_refs:
        x = x + _dot(a_ref[...], mix_w_ref[row:row + a_ref.shape[1], :])
        row += a_ref.shape[1]
    h = _rms_norm(x, g_ref[...]).astype(BF16)
    gate = _dot(h, wg_ref[...])
    up = _dot(h, wu_ref[...])
    act = (gate * jax.nn.sigmoid(gate) * up).astype(BF16)
    y = x + 0.5 * _dot(act, wd_ref[...])
    if final_norm:
        y = _rms_norm(y, fin_ref[...])
    o_ref[...] = y


def _ffn(x, mix, mix_w, g, wg, wu, wd, fin, *, final_norm, tm):
    t, d = x.shape
    row = pl.BlockSpec((tm, d), lambda i: (i, 0))
    mix_specs = [pl.BlockSpec((tm, a.shape[1]), lambda i: (i, 0)) for a in mix]
    return pl.pallas_call(
        functools.partial(_ffn_kernel, n_mix=len(mix), final_norm=final_norm),
        grid=(t // tm,),
        in_specs=[row] + mix_specs + [_resident()] * (len(mix_w) + 5),
        out_specs=row,
        out_shape=jax.ShapeDtypeStruct((t, d), F32),
        compiler_params=_params(1),
        name="ffn_final" if final_norm else ("ffn_mix" if mix else "ffn"),
    )(x, *mix, *mix_w, g, wg, wu, wd, fin)


def _rope_tables(seq_len):
    half = ROT_DIM // 2
    inv = 1.0 / (ROPE_THETA ** (jnp.arange(0, ROT_DIM, 2, dtype=F32) / ROT_DIM))
    ang = jnp.arange(seq_len, dtype=F32)[:, None] * inv[None, :]
    cos, sin = jnp.cos(ang), jnp.sin(ang)
    ones = jnp.ones((seq_len, HEAD_DIM - ROT_DIM), F32)
    zeros_h = jnp.zeros((seq_len, half), F32)
    zeros_r = jnp.zeros((seq_len, HEAD_DIM - ROT_DIM), F32)
    c_same = jnp.concatenate([cos, cos, ones], axis=-1)
    c_prev = jnp.concatenate([zeros_h, sin, zeros_r], axis=-1)
    c_next = jnp.concatenate([-sin, zeros_h, zeros_r], axis=-1)
    rep = LANES // HEAD_DIM
    return tuple(jnp.tile(c, (1, rep)) for c in (c_same, c_prev, c_next))


def _rope(zc, c_same, c_prev, c_next):
    half = ROT_DIM // 2
    return (zc * c_same + pltpu.roll(zc, half, 1) * c_prev
            + pltpu.roll(zc, LANES - half, 1) * c_next)


def _group_mean(a, avg):
    slab = avg.shape[0]
    return jnp.concatenate([_dot(a[:, lo:lo + slab].astype(BF16), avg) for lo in range(0, a.shape[1], slab)], axis=1)


def _even_in_kernel(x_ref, g_ref, w_ref, cs_ref, cp_ref, cn_ref, avg_ref, lng_ref, lnb_ref, ws_ref,
                    bias_ref, q_ref, k_ref, v_ref, km_ref, gated_ref, *, width):
    tm = x_ref.shape[0]
    h = _rms_norm(x_ref[...], g_ref[...]).astype(BF16)
    z = _dot(h, w_ref[...])
    c_same, c_prev, c_next = cs_ref[...], cp_ref[...], cn_ref[...]
    n_lane_blocks = width // LANES
    for cb in range(n_lane_blocks):
        lo = cb * LANES
        q_ref[:, lo:lo + LANES] = (_rope(z[:, lo:lo + LANES], c_same, c_prev, c_next) * QK_SCALE).astype(BF16)
        kr = _rope(z[:, width + lo:width + lo + LANES], c_same, c_prev, c_next)
        k_ref[:, lo:lo + LANES] = kr.astype(BF16)
        for blk in range(tm // MOBA_BLOCK):
            rows = kr[blk * MOBA_BLOCK:(blk + 1) * MOBA_BLOCK]
            km_ref[blk, :, lo:lo + LANES] = jnp.mean(rows, axis=0, keepdims=True)
    v_ref[...] = z[:, 2 * width:3 * width].astype(BF16)

    gz = z[:, 3 * width:]
    gz = 0.5 * gz * (1.0 + jnp.tanh(math.sqrt(2.0 / math.pi) * (gz + 0.044715 * (gz * gz * gz))))
    u, vv = gz[:, :width], gz[:, width:]
    avg = avg_ref[...]
    cen = vv - _group_mean(vv, avg)
    var = _group_mean(cen * cen, avg)
    vn = (cen * lax.rsqrt(var + NORM_EPS) * lng_ref[...] + lnb_ref[...]).astype(BF16)

    t_idx = lax.broadcasted_iota(jnp.int32, (GMLP_CHUNK, GMLP_CHUNK), 0)
    s_idx = lax.broadcasted_iota(jnp.int32, (GMLP_CHUNK, GMLP_CHUNK), 1)
    causal = s_idx <= t_idx
    first_group = lax.broadcasted_iota(jnp.int32, (GMLP_CHUNK, LANES), 1) < GMLP_GROUP_DIM
    chunks = range(0, tm, GMLP_CHUNK)
    for pair in range(n_lane_blocks):
        lo = pair * LANES
        w0 = jnp.where(causal, ws_ref[2 * pair], 0.0).astype(BF16)
        w1 = jnp.where(causal, ws_ref[2 * pair + 1], 0.0).astype(BF16)
        bias = bias_ref[:, lo:lo + LANES]
        vc = jnp.concatenate([vn[r0:r0 + GMLP_CHUNK, lo:lo + LANES] for r0 in chunks], axis=1)
        m0, m1 = _dot(w0, vc), _dot(w1, vc)
        for c, r0 in enumerate(chunks):
            mixed = jnp.where(first_group, m0[:, c * LANES:(c + 1) * LANES], m1[:, c * LANES:(c + 1) * LANES]) + bias
            gated_ref[r0:r0 + GMLP_CHUNK, lo:lo + LANES] = (u[r0:r0 + GMLP_CHUNK, lo:lo + LANES] * mixed).astype(BF16)


def _even_in(x, g, w_in, tables, avg, ln_g, ln_b, w_s, bias, *, seq_len, tm):
    t, d = x.shape
    width = w_in.shape[1] // 5
    n_seq_tiles = seq_len // tm
    row = pl.BlockSpec((tm, d), lambda i: (i, 0))
    tab = pl.BlockSpec((tm, LANES), lambda i: (i % n_seq_tiles, 0))
    out_row = pl.BlockSpec((tm, width), lambda i: (i, 0))
    nb = tm // MOBA_BLOCK
    act = jax.ShapeDtypeStruct((t, width), BF16)
    return pl.pallas_call(
        functools.partial(_even_in_kernel, width=width),
        grid=(t // tm,),
        in_specs=[row, _resident(), _resident(), tab, tab, tab, _resident(), _resident(), _resident(),
                  _resident(), _resident()],
        out_specs=[out_row, out_row, out_row, pl.BlockSpec((nb, 1, width), lambda i: (i, 0, 0)), out_row],
        out_shape=[act, act, act, jax.ShapeDtypeStruct((t // MOBA_BLOCK, 1, width), F32), act],
        compiler_params=_params(1),
        name="even_in",
    )(x, g, w_in, *tables, avg, ln_g, ln_b, w_s, bias)


def _stack_halves(q):
    lane = lax.broadcasted_iota(jnp.int32, q.shape, 1)
    zero = jnp.zeros_like(q)
    return jnp.concatenate([jnp.where(lane < HEAD_DIM, q, zero), jnp.where(lane >= HEAD_DIM, q, zero)], axis=0)


def _flash_scratch(rows, tk):
    scores, probs = pltpu.VMEM((rows, tk), F32), pltpu.VMEM((rows, tk), BF16)
    state = pltpu.VMEM((rows, LANES), F32)
    return [scores, scores, state, state, probs, probs, state, state, state, state, state]


def _lane_blocks(s):
    return [s[:, c * LANES:(c + 1) * LANES] for c in range(s.shape[1] // LANES)]


def _score_stage(qk_fn, t, ranges, s_ref, c_ref):
    for r0, n in ranges:
        s = qk_fn(t, r0, n)
        s_ref[r0:r0 + n, :] = s
        c_ref[r0:r0 + n, :] = functools.reduce(jnp.maximum, _lane_blocks(s))


def _softmax_stage(ranges, s_ref, c_ref, p_ref, alpha_ref, m_ref, l_ref, *, start):
    for r0, n, on_diagonal in ranges:
        rows = slice(r0, r0 + n)
        s = s_ref[rows, :]
        if on_diagonal:
            visible = lax.broadcasted_iota(jnp.int32, s.shape, 1) <= lax.broadcasted_iota(jnp.int32, s.shape, 0)
            cols = _lane_blocks(jnp.where(visible, s, NEG_INF))
            col_max = functools.reduce(jnp.maximum, cols)
        else:
            cols = _lane_blocks(s)
            col_max = c_ref[rows, :]
        m_new = jnp.max(col_max, axis=-1, keepdims=True)
        if start:
            m_new = jnp.broadcast_to(m_new, col_max.shape)
        else:
            m_prev = m_ref[rows, :]
            m_new = jnp.maximum(m_prev, m_new)
            alpha = jnp.exp2(m_prev - m_new)
            alpha_ref[rows, :] = alpha
        ps = [jnp.exp2(c - m_new) for c in cols]
        l_new = functools.reduce(jnp.add, ps)
        l_ref[rows, :] = l_new if start else alpha * l_ref[rows, :] + l_new
        p_ref[rows, :] = jnp.concatenate([p.astype(BF16) for p in ps], axis=1)
        m_ref[rows, :] = m_new


def _pv_stage(v, ranges, p_ref, alpha_ref, acc_ref, *, start):
    for r0, n in ranges:
        rows = slice(r0, r0 + n)
        pv = _dot(p_ref[rows, :], v)
        acc_ref[rows, :] = pv if start else alpha_ref[rows, :] * acc_ref[rows, :] + pv


def _flash_pipeline(qi, tq, qk_fn, v_fn, s0, s1, c0, c1, p0, p1, a0, a1, m_ref, l_ref, acc_ref):
    s_refs, c_refs, p_refs, a_refs = (s0, s1), (c0, c1), (p0, p1), (a0, a1)
    rows, tk = s0.shape
    tiles_per_q = tq // tk
    assert tiles_per_q % 2 == 0 and rows == 2 * tq
    n_past = qi * tiles_per_q
    whole = ((0, rows),)

    def seen_by(d):
        return tuple((half * tq + d * tk, tq - d * tk) for half in range(2))

    def softmax_ranges(d):
        out = []
        for half in range(2):
            out.append((half * tq + d * tk, tk, True))
            if (d + 1) * tk < tq:
                out.append((half * tq + (d + 1) * tk, tq - (d + 1) * tk, False))
        return tuple(out)

    def tick(t, slot, pv_ranges, sm_ranges, next_ranges, *, age):
        if age > 0:
            _pv_stage(v_fn(t - 1), pv_ranges, p_refs[1 - slot], a_refs[1 - slot], acc_ref, start=age == 1)
        _softmax_stage(sm_ranges, s_refs[slot], c_refs[slot], p_refs[slot], a_refs[slot], m_ref, l_ref,
                       start=age == 0)
        if next_ranges:
            _score_stage(qk_fn, t + 1, next_ranges, s_refs[1 - slot], c_refs[1 - slot])

    past = ((0, rows, False),)

    def past_pair(i, age):
        tick(2 * i, 0, whole, past, whole, age=age)
        tick(2 * i + 1, 1, whole, past, whole, age=min(age + 1, 2))

    def diagonal_ticks(age):
        for d in range(tiles_per_q):
            tick(n_past + d, d % 2, whole if d == 0 else seen_by(d - 1), softmax_ranges(d),
                 seen_by(d + 1) if d + 1 < tiles_per_q else (), age=min(age + d, 2))

    _score_stage(qk_fn, 0, whole, s0, c0)

    @pl.when(qi == 0)
    def _():
        diagonal_ticks(0)

    @pl.when(qi > 0)
    def _():
        past_pair(0, 0)
        n_left = qi * (tiles_per_q // 2) - 1
        odd = jnp.bitwise_and(n_left, 1)

        @pl.when(odd == 1)
        def _():
            past_pair(1, 2)

        def body(j, carry):
            i = 1 + odd + 2 * j
            past_pair(i, 2)
            past_pair(i + 1, 2)
            return carry

        lax.fori_loop(0, jnp.right_shift(n_left, 1), body, 0)
        diagonal_ticks(2)

    last = tiles_per_q - 1
    _pv_stage(v_fn(n_past + last), seen_by(last), p_refs[last % 2], a_refs[last % 2], acc_ref, start=False)
    return acc_ref[...] / jnp.sum(l_ref[...], axis=-1, keepdims=True)


def _block_bias(gate, own_block, n_blocks):
    block_id = lax.broadcasted_iota(jnp.int32, gate.shape, 0)
    g = jnp.where(block_id < own_block, gate, NEG_INF)
    bias = jnp.where(block_id < n_blocks, jnp.where(block_id == own_block, 0.0, MASKED), 0.0)
    for _ in range(MOBA_TOPK):
        best = jnp.max(g, axis=0, keepdims=True)
        cand = jnp.where(g == best, jnp.where(best > NEG_INF, block_id, LANES), LANES)
        pick = block_id == jnp.min(cand, axis=0, keepdims=True)
        bias = jnp.where(pick, 0.0, bias)
        g = jnp.where(pick, NEG_INF, g)
    return bias


def _moba_kernel(q_ref, k_ref, v_ref, km_ref, o_ref, *scratch, n_blocks):
    tq = q_ref.shape[1]
    log2_block = MOBA_BLOCK.bit_length() - 1
    blocks_per_tile = tq // MOBA_BLOCK
    qi = pl.program_id(2)
    qs = _stack_halves(q_ref[0])
    n_pad = -(-n_blocks // 8) * 8
    gate = _dot_nt(km_ref[0, :n_pad, :].astype(BF16), qs)
    query = jnp.bitwise_and(lax.broadcasted_iota(jnp.int32, gate.shape, 1), tq - 1)
    own_block = qi * blocks_per_tile + jnp.right_shift(query, log2_block)
    bias = _block_bias(gate, own_block, n_blocks)
    if n_pad < LANES:
        bias = jnp.concatenate([bias, jnp.zeros((LANES - n_pad, 2 * tq), F32)], axis=0)
    bias = bias.T
    q_aug = jnp.concatenate([qs, bias.astype(BF16)], axis=1)

    tk = scratch[0].shape[1]
    key_lane = lax.broadcasted_iota(jnp.int32, (tk, LANES), 1)
    key_block = jnp.right_shift(lax.broadcasted_iota(jnp.int32, (tk, LANES), 0), log2_block)

    def scores(t, r0, n):
        one_hot = jnp.where(key_lane == t * (tk // MOBA_BLOCK) + key_block, 1.0, 0.0).astype(BF16)
        k = k_ref[0, pl.ds(pl.multiple_of(t * tk, tk), tk), :]
        return _dot_nt(q_aug[r0:r0 + n], jnp.concatenate([k, one_hot], axis=1))

    def values(t):
        return v_ref[0, pl.ds(pl.multiple_of(t * tk, tk), tk), :]

    o = _flash_pipeline(qi, tq, scores, values, *scratch)
    out_lane = lax.broadcasted_iota(jnp.int32, (tq, LANES), 1)
    o_ref[0] = jnp.where(out_lane < HEAD_DIM, o[:tq], o[tq:]).astype(BF16)


def _moba(q, k, v, km, *, tq, tk):
    b, s, width = q.shape
    n_blocks = s // MOBA_BLOCK
    assert n_blocks <= LANES and tk % MOBA_BLOCK == 0 and tq & (tq - 1) == 0
    qspec = pl.BlockSpec((1, tq, LANES), lambda bi, hp, qi: (bi, qi, hp))
    kvspec = pl.BlockSpec((1, s, LANES), lambda bi, hp, qi: (bi, 0, hp))
    return pl.pallas_call(
        functools.partial(_moba_kernel, n_blocks=n_blocks),
        grid=(b, width // LANES, s // tq),
        in_specs=[qspec, kvspec, kvspec, pl.BlockSpec((1, LANES, LANES), lambda bi, hp, qi: (bi, 0, hp))],
        out_specs=qspec,
        out_shape=jax.ShapeDtypeStruct((b, s, width), BF16),
        scratch_shapes=_flash_scratch(2 * tq, tk),
        compiler_params=_params(3),
        name="moba_attn",
    )(q, k, v, km)


def _odd_in_kernel(x_ref, g_ref, w_ref, cs_ref, cp_ref, cn_ref, q_ref, k_ref, v_ref, *, qk_width):
    h = _rms_norm(x_ref[...], g_ref[...]).astype(BF16)
    z = _dot(h, w_ref[...])
    c_same, c_prev, c_next = cs_ref[...], cp_ref[...], cn_ref[...]
    for cb in range(qk_width // LANES):
        lo = cb * LANES
        q_ref[:, lo:lo + LANES] = (_rope(z[:, lo:lo + LANES], c_same, c_prev, c_next) * QK_SCALE).astype(BF16)
        k_ref[:, lo:lo + LANES] = _rope(z[:, qk_width + lo:qk_width + lo + LANES], c_same, c_prev, c_next).astype(BF16)
    v_ref[...] = z[:, 2 * qk_width:].astype(BF16)


def _odd_in(x, g, w_in, tables, *, qk_width, seq_len, tm):
    t, d = x.shape
    v_width = w_in.shape[1] - 2 * qk_width
    n_seq_tiles = seq_len // tm
    row = pl.BlockSpec((tm, d), lambda i: (i, 0))
    tab = pl.BlockSpec((tm, LANES), lambda i: (i % n_seq_tiles, 0))
    qk_row = pl.BlockSpec((tm, qk_width), lambda i: (i, 0))
    qk = jax.ShapeDtypeStruct((t, qk_width), BF16)
    return pl.pallas_call(
        functools.partial(_odd_in_kernel, qk_width=qk_width),
        grid=(t // tm,),
        in_specs=[row, _resident(), _resident(), tab, tab, tab],
        out_specs=[qk_row, qk_row, pl.BlockSpec((tm, v_width), lambda i: (i, 0))],
        out_shape=[qk, qk, jax.ShapeDtypeStruct((t, v_width), BF16)],
        compiler_params=_params(1),
        name="odd_in",
    )(x, g, w_in, *tables)


def _diff_kernel(lq1_ref, lk1_ref, lq2_ref, lk2_ref, g_ref, q_ref, k_ref, v_ref, o_ref, *scratch, lambda_init):
    tq = q_ref.shape[1]
    qs = _stack_halves(q_ref[0])

    tk = scratch[0].shape[1]

    def scores(t, r0, n):
        return _dot_nt(qs[r0:r0 + n], k_ref[0, pl.ds(pl.multiple_of(t * tk, tk), tk), :])

    def values(t):
        return v_ref[0, pl.ds(pl.multiple_of(t * tk, tk), tk), :]

    o = _flash_pipeline(pl.program_id(2), tq, scores, values, *scratch)
    lam = (jnp.exp(jnp.sum(lq1_ref[...] * lk1_ref[...], axis=-1, keepdims=True))
           - jnp.exp(jnp.sum(lq2_ref[...] * lk2_ref[...], axis=-1, keepdims=True)) + lambda_init)
    o = o[:tq] - lam * o[tq:]
    o_ref[0] = (_rms_norm(o, g_ref[...]) * (1.0 - lambda_init)).astype(BF16)


def _diff_attn(q, k, v, lq1, lk1, lq2, lk2, subln_g, *, lambda_init, tq, tk):
    b, s, width = q.shape
    qspec = pl.BlockSpec((1, tq, LANES), lambda bi, h, qi: (bi, qi, h))
    kvspec = pl.BlockSpec((1, s, LANES), lambda bi, h, qi: (bi, 0, h))
    return pl.pallas_call(
        functools.partial(_diff_kernel, lambda_init=lambda_init),
        grid=(b, width // LANES, s // tq),
        in_specs=[_resident()] * 5 + [qspec, kvspec, kvspec],
        out_specs=qspec,
        out_shape=jax.ShapeDtypeStruct((b, s, width), BF16),
        scratch_shapes=_flash_scratch(2 * tq, tk),
        compiler_params=_params(3),
        name="diff_attn",
    )(lq1, lk1, lq2, lk2, subln_g, q, k, v)


def kernel(x, ffn_pre_norm, ffn_pre_w_gate, ffn_pre_w_up, ffn_pre_w_down, mix_norm, ffn_post_norm, ffn_post_w_gate, ffn_post_w_up, ffn_post_w_down, even_w_in, even_w_out, gmlp_ln_g, gmlp_ln_b, gmlp_w_s, gmlp_b_s, odd_w_in, odd_w_out, diff_lambda_q1, diff_lambda_k1, diff_lambda_q2, diff_lambda_k2, diff_subln_g, final_norm):
    b, s, d = x.shape
    depth = ffn_pre_norm.shape[0]
    t = b * s
    tm = 512
    tables = _rope_tables(s)
    xf = x.reshape(t, d)
    fin = final_norm.reshape(1, d)

    def ffn(xf, mix, mix_w, layer, norm, wg, wu, wd, final):
        return _ffn(xf, mix, mix_w, norm[layer].reshape(1, d), _layer_bf16(wg, layer), _layer_bf16(wu, layer),
                    _layer_bf16(wd, layer), fin, final_norm=final, tm=tm)

    for layer in range(depth):
        xf = ffn(xf, [], [], layer, ffn_pre_norm, ffn_pre_w_gate, ffn_pre_w_up, ffn_pre_w_down, False)
        g_mix = mix_norm[layer].reshape(1, d)
        if layer % 2 == 0:
            e = layer // 2
            width = GMLP_GROUPS * GMLP_GROUP_DIM
            n_blocks = s // MOBA_BLOCK
            group = jnp.arange(2 * LANES) // GMLP_GROUP_DIM
            avg = (group[:, None] == group[None, :]).astype(BF16) / GMLP_GROUP_DIM
            bias = jnp.repeat(gmlp_b_s[e].T, GMLP_GROUP_DIM, axis=1)
            q, k, v, km, gated = _even_in(
                xf, g_mix, _layer_bf16(even_w_in, e), tables, avg, gmlp_ln_g[e].reshape(1, width),
                gmlp_ln_b[e].reshape(1, width), gmlp_w_s[e], bias, seq_len=s, tm=tm)
            km = jnp.pad(km.reshape(b, n_blocks, width), ((0, 0), (0, LANES - n_blocks), (0, 0)))
            attn = _moba(q.reshape(b, s, width), k.reshape(b, s, width), v.reshape(b, s, width), km,
                         tq=min(ATTN_Q_TILE, s), tk=ATTN_KV_TILE)
            mix, mix_w = [attn.reshape(t, width), gated], [_layer_bf16(even_w_out, e)]
        else:
            o = layer // 2
            lambda_init = 0.8 - 0.6 * math.exp(-0.3 * layer)
            v_width = odd_w_out.shape[1]
            qk_width = (odd_w_in.shape[2] - v_width) // 2
            q, k, v = _odd_in(xf, g_mix, _layer_bf16(odd_w_in, o), tables, qk_width=qk_width, seq_len=s, tm=tm)
            vec = lambda a: a[o].reshape(1, -1)
            attn = _diff_attn(q.reshape(b, s, qk_width), k.reshape(b, s, qk_width), v.reshape(b, s, v_width),
                              vec(diff_lambda_q1), vec(diff_lambda_k1), vec(diff_lambda_q2), vec(diff_lambda_k2),
                              vec(diff_subln_g), lambda_init=lambda_init, tq=min(ATTN_Q_TILE, s),
                              tk=ATTN_KV_TILE)
            mix, mix_w = [attn.reshape(t, v_width)], [_layer_bf16(odd_w_out, o)]
        xf = ffn(xf, mix, mix_w, layer, ffn_post_norm, ffn_post_w_gate, ffn_post_w_up, ffn_post_w_down,
                 layer == depth - 1)
    return xf.reshape(b, s, d)
```

```python
import functools
import math

import jax
import jax.numpy as jnp
from jax import lax
from jax.experimental import pallas as pl
from jax.experimental.pallas import tpu as pltpu

F32 = jnp.float32
BF16 = jnp.bfloat16

HEAD_DIM = 64
ROT_DIM = HEAD_DIM // 4
ROPE_THETA = 500000.0
NORM_EPS = 1e-6
MOBA_BLOCK = 256
MOBA_TOPK = 3
GMLP_GROUPS = 8
GMLP_GROUP_DIM = 64
GMLP_CHUNK = 128

LANES = 128
VMEM_LIMIT_BYTES = 56 * 1024 * 1024
ATTN_Q_TILE = 1024
ATTN_KV_TILE = 512

NEG_INF = float("-inf")
MASKED = -1e30
QK_SCALE = HEAD_DIM ** -0.5 * math.log2(math.e)


def _params(n_grid_axes):
    return pltpu.CompilerParams(
        dimension_semantics=("parallel",) * (n_grid_axes - 1) + ("arbitrary",),
        vmem_limit_bytes=VMEM_LIMIT_BYTES,
    )


def _resident():
    return pl.BlockSpec(memory_space=pltpu.VMEM)


def _rms_norm(x, g):
    ms = jnp.mean(x * x, axis=-1, keepdims=True)
    return x * lax.rsqrt(ms + NORM_EPS) * g


def _dot(a, b):
    return jnp.dot(a, b, preferred_element_type=F32)


def _dot_nt(a, b):
    return lax.dot_general(a, b, (((1,), (1,)), ((), ())), preferred_element_type=F32)


def _cast_kernel(x_ref, o_ref):
    o_ref[...] = x_ref[...].astype(o_ref.dtype)


def _layer_bf16(w, layer):
    _, rows, cols = w.shape
    budget = 3 * 1024 * 1024
    rb = max(r for r in range(16, rows + 1, 16) if rows % r == 0 and r * cols * 4 <= budget)
    return pl.pallas_call(
        _cast_kernel,
        grid=(rows // rb,),
        in_specs=[pl.BlockSpec((None, rb, cols), lambda i: (layer, i, 0))],
        out_specs=pl.BlockSpec((rb, cols), lambda i: (i, 0)),
        out_shape=jax.ShapeDtypeStruct((rows, cols), BF16),
        compiler_params=_params(1),
        name="to_bf16",
    )(w)


def _ffn_kernel(*refs, n_mix, final_norm):
    x_ref, mix_refs = refs[0], refs[1:1 + n_mix]
    mix_w_ref = refs[1 + n_mix] if n_mix else None
    g_ref, wg_ref, wu_ref, wd_ref, fin_ref, o_ref = refs[1 + n_mix + (1 if n_mix else 0):]
    x = x_ref[...]
    row = 0
    for a_ref in mix_refs:
        x = x + _dot(a_ref[...], mix_w_ref[row:row + a_ref.shape[1], :])
        row += a_ref.shape[1]
    h = _rms_norm(x, g_ref[...]).astype(BF16)
    gate = _dot(h, wg_ref[...])
    up = _dot(h, wu_ref[...])
    act = (gate * jax.nn.sigmoid(gate) * up).astype(BF16)
    y = x + 0.5 * _dot(act, wd_ref[...])
    if final_norm:
        y = _rms_norm(y, fin_ref[...])
    o_ref[...] = y


def _ffn(x, mix, mix_w, g, wg, wu, wd, fin, *, final_norm, tm):
    t, d = x.shape
    row = pl.BlockSpec((tm, d), lambda i: (i, 0))
    mix_specs = [pl.BlockSpec((tm, a.shape[1]), lambda i: (i, 0)) for a in mix]
    return pl.pallas_call(
        functools.partial(_ffn_kernel, n_mix=len(mix), final_norm=final_norm),
        grid=(t // tm,),
        in_specs=[row] + mix_specs + [_resident()] * (len(mix_w) + 5),
        out_specs=row,
        out_shape=jax.ShapeDtypeStruct((t, d), F32),
        compiler_params=_params(1),
        name="ffn_final" if final_norm else ("ffn_mix" if mix else "ffn"),
    )(x, *mix, *mix_w, g, wg, wu, wd, fin)


def _rope_tables(seq_len):
    half = ROT_DIM // 2
    inv = 1.0 / (ROPE_THETA ** (jnp.arange(0, ROT_DIM, 2, dtype=F32) / ROT_DIM))
    ang = jnp.arange(seq_len, dtype=F32)[:, None] * inv[None, :]
    cos, sin = jnp.cos(ang), jnp.sin(ang)
    ones = jnp.ones((seq_len, HEAD_DIM - ROT_DIM), F32)
    zeros_h = jnp.zeros((seq_len, half), F32)
    zeros_r = jnp.zeros((seq_len, HEAD_DIM - ROT_DIM), F32)
    c_same = jnp.concatenate([cos, cos, ones], axis=-1)
    c_prev = jnp.concatenate([zeros_h, sin, zeros_r], axis=-1)
    c_next = jnp.concatenate([-sin, zeros_h, zeros_r], axis=-1)
    rep = LANES // HEAD_DIM
    return tuple(jnp.tile(c, (1, rep)) for c in (c_same, c_prev, c_next))


def _rope(zc, c_same, c_prev, c_next):
    half = ROT_DIM // 2
    return (zc * c_same + pltpu.roll(zc, half, 1) * c_prev
            + pltpu.roll(zc, LANES - half, 1) * c_next)


def _group_mean(a, avg):
    slab = avg.shape[0]
    return jnp.concatenate([_dot(a[:, lo:lo + slab].astype(BF16), avg) for lo in range(0, a.shape[1], slab)], axis=1)


def _even_in_kernel(x_ref, g_ref, w_ref, cs_ref, cp_ref, cn_ref, avg_ref, lng_ref, lnb_ref, ws_ref,
                    bias_ref, q_ref, k_ref, v_ref, km_ref, gated_ref, *, width):
    tm = x_ref.shape[0]
    h = _rms_norm(x_ref[...], g_ref[...]).astype(BF16)
    z = _dot(h, w_ref[...])
    c_same, c_prev, c_next = cs_ref[...], cp_ref[...], cn_ref[...]
    n_lane_blocks = width // LANES
    for cb in range(n_lane_blocks):
        lo = cb * LANES
        q_ref[:, lo:lo + LANES] = (_rope(z[:, lo:lo + LANES], c_same, c_prev, c_next) * QK_SCALE).astype(BF16)
        kr = _rope(z[:, width + lo:width + lo + LANES], c_same, c_prev, c_next)
        k_ref[:, lo:lo + LANES] = kr.astype(BF16)
        for blk in range(tm // MOBA_BLOCK):
            rows = kr[blk * MOBA_BLOCK:(blk + 1) * MOBA_BLOCK]
            km_ref[blk, :, lo:lo + LANES] = jnp.mean(rows, axis=0, keepdims=True)
    v_ref[...] = z[:, 2 * width:3 * width].astype(BF16)

    gz = z[:, 3 * width:]
    gz = 0.5 * gz * (1.0 + jnp.tanh(math.sqrt(2.0 / math.pi) * (gz + 0.044715 * (gz * gz * gz))))
    u, vv = gz[:, :width], gz[:, width:]
    avg = avg_ref[...]
    cen = vv - _group_mean(vv, avg)
    var = _group_mean(cen * cen, avg)
    vn = (cen * lax.rsqrt(var + NORM_EPS) * lng_ref[...] + lnb_ref[...]).astype(BF16)

    t_idx = lax.broadcasted_iota(jnp.int32, (GMLP_CHUNK, GMLP_CHUNK), 0)
    s_idx = lax.broadcasted_iota(jnp.int32, (GMLP_CHUNK, GMLP_CHUNK), 1)
    causal = s_idx <= t_idx
    first_group = lax.broadcasted_iota(jnp.int32, (GMLP_CHUNK, LANES), 1) < GMLP_GROUP_DIM
    chunks = range(0, tm, GMLP_CHUNK)
    for pair in range(n_lane_blocks):
        lo = pair * LANES
        w0 = jnp.where(causal, ws_ref[2 * pair], 0.0).astype(BF16)
        w1 = jnp.where(causal, ws_ref[2 * pair + 1], 0.0).astype(BF16)
        bias = bias_ref[:, lo:lo + LANES]
        vc = jnp.concatenate([vn[r0:r0 + GMLP_CHUNK, lo:lo + LANES] for r0 in chunks], axis=1)
        m0, m1 = _dot(w0, vc), _dot(w1, vc)
        for c, r0 in enumerate(chunks):
            mixed = jnp.where(first_group, m0[:, c * LANES:(c + 1) * LANES], m1[:, c * LANES:(c + 1) * LANES]) + bias
            gated_ref[r0:r0 + GMLP_CHUNK, lo:lo + LANES] = (u[r0:r0 + GMLP_CHUNK, lo:lo + LANES] * mixed).astype(BF16)


def _even_in(x, g, w_in, tables, avg, ln_g, ln_b, w_s, bias, *, seq_len, tm):
    t, d = x.shape
    width = w_in.shape[1] // 5
    n_seq_tiles = seq_len // tm
    row = pl.BlockSpec((tm, d), lambda i: (i, 0))
    tab = pl.BlockSpec((tm, LANES), lambda i: (i % n_seq_tiles, 0))
    out_row = pl.BlockSpec((tm, width), lambda i: (i, 0))
    nb = tm // MOBA_BLOCK
    act = jax.ShapeDtypeStruct((t, width), BF16)
    return pl.pallas_call(
        functools.partial(_even_in_kernel, width=width),
        grid=(t // tm,),
        in_specs=[row, _resident(), _resident(), tab, tab, tab, _resident(), _resident(), _resident(),
                  _resident(), _resident()],
        out_specs=[out_row, out_row, out_row, pl.BlockSpec((nb, 1, width), lambda i: (i, 0, 0)), out_row],
        out_shape=[act, act, act, jax.ShapeDtypeStruct((t // MOBA_BLOCK, 1, width), F32), act],
        compiler_params=_params(1),
        name="even_in",
    )(x, g, w_in, *tables, avg, ln_g, ln_b, w_s, bias)


def _stack_halves(q):
    lane = lax.broadcasted_iota(jnp.int32, q.shape, 1)
    zero = jnp.zeros_like(q)
    return jnp.concatenate([jnp.where(lane < HEAD_DIM, q, zero), jnp.where(lane >= HEAD_DIM, q, zero)], axis=0)


def _flash_scratch(rows, tk, *, row_sums):
    scores, probs = pltpu.VMEM((rows, tk), F32), pltpu.VMEM((rows, tk), BF16)
    state = pltpu.VMEM((rows, LANES), F32)
    return [scores, scores, state, state, probs, probs, state, state, state, state] + [state] * row_sums


def _lane_blocks(s):
    return [s[:, c * LANES:(c + 1) * LANES] for c in range(s.shape[1] // LANES)]


def _score_stage(qk_fn, t, ranges, s_ref, c_ref):
    for r0, n in ranges:
        s = qk_fn(t, r0, n)
        s_ref[r0:r0 + n, :] = s
        c_ref[r0:r0 + n, :] = functools.reduce(jnp.maximum, _lane_blocks(s))


def _softmax_stage(ranges, s_ref, c_ref, p_ref, alpha_ref, m_ref, l_ref, *, start):
    for r0, n, on_diagonal in ranges:
        rows = slice(r0, r0 + n)
        s = s_ref[rows, :]
        if on_diagonal:
            visible = lax.broadcasted_iota(jnp.int32, s.shape, 1) <= lax.broadcasted_iota(jnp.int32, s.shape, 0)
            cols = _lane_blocks(jnp.where(visible, s, NEG_INF))
            col_max = functools.reduce(jnp.maximum, cols)
        else:
            cols = _lane_blocks(s)
            col_max = c_ref[rows, :]
        m_new = jnp.max(col_max, axis=-1, keepdims=True)
        if start:
            m_new = jnp.broadcast_to(m_new, col_max.shape)
        else:
            m_prev = m_ref[rows, :]
            m_new = jnp.maximum(m_prev, m_new)
            alpha = jnp.exp2(m_prev - m_new)
            alpha_ref[rows, :] = alpha
        if l_ref is None:
            p_ref[rows, :] = jnp.concatenate([jnp.exp2((c - m_new).astype(BF16)) for c in cols], axis=1)
        else:
            ps = [jnp.exp2(c - m_new) for c in cols]
            l_new = functools.reduce(jnp.add, ps)
            l_ref[rows, :] = l_new if start else alpha * l_ref[rows, :] + l_new
            p_ref[rows, :] = jnp.concatenate([p.astype(BF16) for p in ps], axis=1)
        m_ref[rows, :] = m_new


def _pv_stage(v_fn, t, ranges, p_ref, alpha_ref, acc_ref, *, start):
    for r0, n in ranges:
        rows = slice(r0, r0 + n)
        pv = _dot(p_ref[rows, :], v_fn(t, r0))
        acc_ref[rows, :] = pv if start else alpha_ref[rows, :] * acc_ref[rows, :] + pv


def _flash_pipeline(qi, tq, qk_fn, v_fn, s0, s1, c0, c1, p0, p1, a0, a1, m_ref, acc_ref, l_ref=None):
    s_refs, c_refs, p_refs, a_refs = (s0, s1), (c0, c1), (p0, p1), (a0, a1)
    rows, tk = s0.shape
    tiles_per_q = tq // tk
    assert tiles_per_q % 2 == 0 and rows == 2 * tq
    n_past = qi * tiles_per_q
    whole = ((0, rows),)
    pv_whole = whole if l_ref is not None else ((0, tq), (tq, tq))

    def seen_by(d):
        return tuple((half * tq + d * tk, tq - d * tk) for half in range(2))

    def softmax_ranges(d):
        out = []
        for half in range(2):
            out.append((half * tq + d * tk, tk, True))
            if (d + 1) * tk < tq:
                out.append((half * tq + (d + 1) * tk, tq - (d + 1) * tk, False))
        return tuple(out)

    def tick(t, slot, pv_ranges, sm_ranges, next_ranges, *, age):
        if age > 0:
            _pv_stage(v_fn, t - 1, pv_ranges, p_refs[1 - slot], a_refs[1 - slot], acc_ref, start=age == 1)
        _softmax_stage(sm_ranges, s_refs[slot], c_refs[slot], p_refs[slot], a_refs[slot], m_ref, l_ref,
                       start=age == 0)
        if next_ranges:
            _score_stage(qk_fn, t + 1, next_ranges, s_refs[1 - slot], c_refs[1 - slot])

    past = ((0, rows, False),)

    def past_pair(i, age):
        tick(2 * i, 0, pv_whole, past, whole, age=age)
        tick(2 * i + 1, 1, pv_whole, past, whole, age=min(age + 1, 2))

    def diagonal_ticks(age):
        for d in range(tiles_per_q):
            tick(n_past + d, d % 2, pv_whole if d == 0 else seen_by(d - 1), softmax_ranges(d),
                 seen_by(d + 1) if d + 1 < tiles_per_q else (), age=min(age + d, 2))

    last = tiles_per_q - 1

    def last_pv():
        _pv_stage(v_fn, n_past + last, seen_by(last), p_refs[last % 2], a_refs[last % 2], acc_ref, start=False)

    @pl.when(qi == 0)
    def _():
        _score_stage(qk_fn, 0, whole, s0, c0)
        diagonal_ticks(0)
        last_pv()

    @pl.when(qi > 0)
    def _():
        _score_stage(qk_fn, 0, whole, s0, c0)
        past_pair(0, 0)
        n_left = qi * (tiles_per_q // 2) - 1
        odd = jnp.bitwise_and(n_left, 1)

        @pl.when(odd == 1)
        def _():
            past_pair(1, 2)

        def body(j, carry):
            i = 1 + odd + 2 * j
            past_pair(i, 2)
            past_pair(i + 1, 2)
            return carry

        lax.fori_loop(0, jnp.right_shift(n_left, 1), body, 0)
        diagonal_ticks(2)
        last_pv()

    if l_ref is None:
        return acc_ref[...]
    return acc_ref[...] / jnp.sum(l_ref[...], axis=-1, keepdims=True)


def _block_bias(gate, own_block, n_blocks):
    block_id = lax.broadcasted_iota(jnp.int32, gate.shape, 0)
    g = jnp.where(block_id < own_block, gate, NEG_INF)
    bias = jnp.where(block_id < n_blocks, jnp.where(block_id == own_block, 0.0, MASKED), 0.0)
    for _ in range(MOBA_TOPK):
        best = jnp.max(g, axis=0, keepdims=True)
        cand = jnp.where(g == best, jnp.where(best > NEG_INF, block_id, LANES), LANES)
        pick = block_id == jnp.min(cand, axis=0, keepdims=True)
        bias = jnp.where(pick, 0.0, bias)
        g = jnp.where(pick, NEG_INF, g)
    return bias


def _moba_kernel(q_ref, k_ref, v_ref, km_ref, o_ref, *scratch, n_blocks):
    tq = q_ref.shape[1]
    log2_block = MOBA_BLOCK.bit_length() - 1
    blocks_per_tile = tq // MOBA_BLOCK
    qi = pl.program_id(2)
    qs = _stack_halves(q_ref[0])
    n_pad = -(-n_blocks // 8) * 8
    gate = _dot_nt(km_ref[0, :n_pad, :].astype(BF16), qs)
    query = jnp.bitwise_and(lax.broadcasted_iota(jnp.int32, gate.shape, 1), tq - 1)
    own_block = qi * blocks_per_tile + jnp.right_shift(query, log2_block)
    bias = _block_bias(gate, own_block, n_blocks)
    if n_pad < LANES:
        bias = jnp.concatenate([bias, jnp.zeros((LANES - n_pad, 2 * tq), F32)], axis=0)
    bias = bias.T
    q_aug = jnp.concatenate([qs, bias.astype(BF16)], axis=1)

    tk = scratch[0].shape[1]
    key_lane = lax.broadcasted_iota(jnp.int32, (tk, LANES), 1)
    key_block = jnp.right_shift(lax.broadcasted_iota(jnp.int32, (tk, LANES), 0), log2_block)

    def scores(t, r0, n):
        one_hot = jnp.where(key_lane == t * (tk // MOBA_BLOCK) + key_block, 1.0, 0.0).astype(BF16)
        k = k_ref[0, pl.ds(pl.multiple_of(t * tk, tk), tk), :]
        return _dot_nt(q_aug[r0:r0 + n], jnp.concatenate([k, one_hot], axis=1))

    def values(t, r0):
        v = v_ref[0, pl.ds(pl.multiple_of(t * tk, tk), tk), :]
        own = key_lane < HEAD_DIM if r0 < tq else key_lane >= HEAD_DIM
        ones_lane = HEAD_DIM if r0 < tq else 0
        return jnp.where(own, v, jnp.where(key_lane == ones_lane, 1.0, 0.0).astype(BF16))

    acc = _flash_pipeline(qi, tq, scores, values, *scratch)
    out_lane = lax.broadcasted_iota(jnp.int32, (tq, LANES), 1)
    o0 = acc[:tq] / acc[:tq, HEAD_DIM:HEAD_DIM + 1]
    o1 = acc[tq:] / acc[tq:, 0:1]
    o_ref[0] = jnp.where(out_lane < HEAD_DIM, o0, o1).astype(BF16)


def _moba(q, k, v, km, *, tq, tk):
    b, s, width = q.shape
    n_blocks = s // MOBA_BLOCK
    assert n_blocks <= LANES and tk % MOBA_BLOCK == 0 and tq & (tq - 1) == 0
    qspec = pl.BlockSpec((1, tq, LANES), lambda bi, hp, qi: (bi, qi, hp))
    kvspec = pl.BlockSpec((1, s, LANES), lambda bi, hp, qi: (bi, 0, hp))
    return pl.pallas_call(
        functools.partial(_moba_kernel, n_blocks=n_blocks),
        grid=(b, width // LANES, s // tq),
        in_specs=[qspec, kvspec, kvspec, pl.BlockSpec((1, LANES, LANES), lambda bi, hp, qi: (bi, 0, hp))],
        out_specs=qspec,
        out_shape=jax.ShapeDtypeStruct((b, s, width), BF16),
        scratch_shapes=_flash_scratch(2 * tq, tk, row_sums=False),
        compiler_params=_params(3),
        name="moba_attn",
    )(q, k, v, km)


def _odd_in_kernel(x_ref, g_ref, w_ref, cs_ref, cp_ref, cn_ref, q_ref, k_ref, v_ref, *, qk_width):
    h = _rms_norm(x_ref[...], g_ref[...]).astype(BF16)
    z = _dot(h, w_ref[...])
    c_same, c_prev, c_next = cs_ref[...], cp_ref[...], cn_ref[...]
    for cb in range(qk_width // LANES):
        lo = cb * LANES
        q_ref[:, lo:lo + LANES] = (_rope(z[:, lo:lo + LANES], c_same, c_prev, c_next) * QK_SCALE).astype(BF16)
        k_ref[:, lo:lo + LANES] = _rope(z[:, qk_width + lo:qk_width + lo + LANES], c_same, c_prev, c_next).astype(BF16)
    v_ref[...] = z[:, 2 * qk_width:].astype(BF16)


def _odd_in(x, g, w_in, tables, *, qk_width, seq_len, tm):
    t, d = x.shape
    v_width = w_in.shape[1] - 2 * qk_width
    n_seq_tiles = seq_len // tm
    row = pl.BlockSpec((tm, d), lambda i: (i, 0))
    tab = pl.BlockSpec((tm, LANES), lambda i: (i % n_seq_tiles, 0))
    qk_row = pl.BlockSpec((tm, qk_width), lambda i: (i, 0))
    qk = jax.ShapeDtypeStruct((t, qk_width), BF16)
    return pl.pallas_call(
        functools.partial(_odd_in_kernel, qk_width=qk_width),
        grid=(t // tm,),
        in_specs=[row, _resident(), _resident(), tab, tab, tab],
        out_specs=[qk_row, qk_row, pl.BlockSpec((tm, v_width), lambda i: (i, 0))],
        out_shape=[qk, qk, jax.ShapeDtypeStruct((t, v_width), BF16)],
        compiler_params=_params(1),
        name="odd_in",
    )(x, g, w_in, *tables)


def _diff_kernel(lq1_ref, lk1_ref, lq2_ref, lk2_ref, g_ref, q_ref, k_ref, v_ref, o_ref, *scratch, lambda_init):
    tq = q_ref.shape[1]
    qs = _stack_halves(q_ref[0])

    tk = scratch[0].shape[1]

    def scores(t, r0, n):
        return _dot_nt(qs[r0:r0 + n], k_ref[0, pl.ds(pl.multiple_of(t * tk, tk), tk), :])

    def values(t, r0):
        return v_ref[0, pl.ds(pl.multiple_of(t * tk, tk), tk), :]

    o = _flash_pipeline(pl.program_id(2), tq, scores, values, *scratch)
    lam = (jnp.exp(jnp.sum(lq1_ref[...] * lk1_ref[...], axis=-1, keepdims=True))
           - jnp.exp(jnp.sum(lq2_ref[...] * lk2_ref[...], axis=-1, keepdims=True)) + lambda_init)
    o = o[:tq] - lam * o[tq:]
    o_ref[0] = (_rms_norm(o, g_ref[...]) * (1.0 - lambda_init)).astype(BF16)


def _diff_attn(q, k, v, lq1, lk1, lq2, lk2, subln_g, *, lambda_init, tq, tk):
    b, s, width = q.shape
    qspec = pl.BlockSpec((1, tq, LANES), lambda bi, h, qi: (bi, qi, h))
    kvspec = pl.BlockSpec((1, s, LANES), lambda bi, h, qi: (bi, 0, h))
    return pl.pallas_call(
        functools.partial(_diff_kernel, lambda_init=lambda_init),
        grid=(b, width // LANES, s // tq),
        in_specs=[_resident()] * 5 + [qspec, kvspec, kvspec],
        out_specs=qspec,
        out_shape=jax.ShapeDtypeStruct((b, s, width), BF16),
        scratch_shapes=_flash_scratch(2 * tq, tk, row_sums=True),
        compiler_params=_params(3),
        name="diff_attn",
    )(lq1, lk1, lq2, lk2, subln_g, q, k, v)


def kernel(x, ffn_pre_norm, ffn_pre_w_gate, ffn_pre_w_up, ffn_pre_w_down, mix_norm, ffn_post_norm, ffn_post_w_gate, ffn_post_w_up, ffn_post_w_down, even_w_in, even_w_out, gmlp_ln_g, gmlp_ln_b, gmlp_w_s, gmlp_b_s, odd_w_in, odd_w_out, diff_lambda_q1, diff_lambda_k1, diff_lambda_q2, diff_lambda_k2, diff_subln_g, final_norm):
    b, s, d = x.shape
    depth = ffn_pre_norm.shape[0]
    t = b * s
    tm = 512
    tables = _rope_tables(s)
    xf = x.reshape(t, d)
    fin = final_norm.reshape(1, d)

    def ffn(xf, mix, mix_w, layer, norm, wg, wu, wd, final):
        return _ffn(xf, mix, mix_w, norm[layer].reshape(1, d), _layer_bf16(wg, layer), _layer_bf16(wu, layer),
                    _layer_bf16(wd, layer), fin, final_norm=final, tm=tm)

    for layer in range(depth):
        xf = ffn(xf, [], [], layer, ffn_pre_norm, ffn_pre_w_gate, ffn_pre_w_up, ffn_pre_w_down, False)
        g_mix = mix_norm[layer].reshape(1, d)
        if layer % 2 == 0:
            e = layer // 2
            width = GMLP_GROUPS * GMLP_GROUP_DIM
            n_blocks = s // MOBA_BLOCK
            group = jnp.arange(2 * LANES) // GMLP_GROUP_DIM
            avg = (group[:, None] == group[None, :]).astype(BF16) / GMLP_GROUP_DIM
            bias = jnp.repeat(gmlp_b_s[e].T, GMLP_GROUP_DIM, axis=1)
            q, k, v, km, gated = _even_in(
                xf, g_mix, _layer_bf16(even_w_in, e), tables, avg, gmlp_ln_g[e].reshape(1, width),
                gmlp_ln_b[e].reshape(1, width), gmlp_w_s[e], bias, seq_len=s, tm=tm)
            km = jnp.pad(km.reshape(b, n_blocks, width), ((0, 0), (0, LANES - n_blocks), (0, 0)))
            attn = _moba(q.reshape(b, s, width), k.reshape(b, s, width), v.reshape(b, s, width), km,
                         tq=min(ATTN_Q_TILE, s), tk=ATTN_KV_TILE)
            mix, mix_w = [attn.reshape(t, width), gated], [_layer_bf16(even_w_out, e)]
        else:
            o = layer // 2
            lambda_init = 0.8 - 0.6 * math.exp(-0.3 * layer)
            v_width = odd_w_out.shape[1]
            qk_width = (odd_w_in.shape[2] - v_width) // 2
            q, k, v = _odd_in(xf, g_mix, _layer_bf16(odd_w_in, o), tables, qk_width=qk_width, seq_len=s, tm=tm)
            vec = lambda a: a[o].reshape(1, -1)
            attn = _diff_attn(q.reshape(b, s, qk_width), k.reshape(b, s, qk_width), v.reshape(b, s, v_width),
                              vec(diff_lambda_q1), vec(diff_lambda_k1), vec(diff_lambda_q2), vec(diff_lambda_k2),
                              vec(diff_subln_g), lambda_init=lambda_init, tq=min(ATTN_Q_TILE, s),
                              tk=ATTN_KV_TILE)
            mix, mix_w = [attn.reshape(t, v_width)], [_layer_bf16(odd_w_out, o)]
        xf = ffn(xf, mix, mix_w, layer, ffn_post_norm, ffn_post_w_gate, ffn_post_w_up, ffn_post_w_down,
                 layer == depth - 1)
    return xf.reshape(b, s, d)
```

```python
import functools
import math

import jax
import jax.numpy as jnp
from jax import lax
from jax.experimental import pallas as pl
from jax.experimental.pallas import tpu as pltpu

F32 = jnp.float32
BF16 = jnp.bfloat16

HEAD_DIM = 64
ROT_DIM = HEAD_DIM // 4
ROPE_THETA = 500000.0
NORM_EPS = 1e-6
MOBA_BLOCK = 256
MOBA_TOPK = 3
GMLP_GROUPS = 8
GMLP_GROUP_DIM = 64
GMLP_CHUNK = 128

LANES = 128
VMEM_LIMIT_BYTES = 56 * 1024 * 1024
MOBA_Q_TILE = 2048
DIFF_Q_TILE = 1024
ATTN_KV_TILE = 512

NEG_INF = float("-inf")
MASKED = -1e30
QK_SCALE = HEAD_DIM ** -0.5 * math.log2(math.e)


def _params(n_grid_axes):
    return pltpu.CompilerParams(
        dimension_semantics=("parallel",) * (n_grid_axes - 1) + ("arbitrary",),
        vmem_limit_bytes=VMEM_LIMIT_BYTES,
    )


def _resident():
    return pl.BlockSpec(memory_space=pltpu.VMEM)


def _rms_norm(x, g):
    ms = jnp.mean(x * x, axis=-1, keepdims=True)
    return x * lax.rsqrt(ms + NORM_EPS) * g


def _dot(a, b):
    return jnp.dot(a, b, preferred_element_type=F32)


def _dot_nt(a, b):
    return lax.dot_general(a, b, (((1,), (1,)), ((), ())), preferred_element_type=F32)


def _cast_kernel(x_ref, o_ref):
    o_ref[...] = x_ref[...].astype(o_ref.dtype)


def _layer_bf16(w, layer):
    _, rows, cols = w.shape
    budget = 3 * 1024 * 1024
    rb = max(r for r in range(16, rows + 1, 16) if rows % r == 0 and r * cols * 4 <= budget)
    return pl.pallas_call(
        _cast_kernel,
        grid=(rows // rb,),
        in_specs=[pl.BlockSpec((None, rb, cols), lambda i: (layer, i, 0))],
        out_specs=pl.BlockSpec((rb, cols), lambda i: (i, 0)),
        out_shape=jax.ShapeDtypeStruct((rows, cols), BF16),
        compiler_params=_params(1),
        name="to_bf16",
    )(w)


def _ffn_kernel(*refs, n_mix, final_norm):
    x_ref, mix_refs = refs[0], refs[1:1 + n_mix]
    mix_w_ref = refs[1 + n_mix] if n_mix else None
    g_ref, wg_ref, wu_ref, wd_ref, fin_ref, o_ref = refs[1 + n_mix + (1 if n_mix else 0):]
    x = x_ref[...]
    row = 0
    for a_ref in mix_refs:
        x = x + _dot(a_ref[...], mix_w_ref[row:row + a_ref.shape[1], :])
        row += a_ref.shape[1]
    h = _rms_norm(x, g_ref[...]).astype(BF16)
    gate = _dot(h, wg_ref[...])
    up = _dot(h, wu_ref[...])
    act = (gate * jax.nn.sigmoid(gate) * up).astype(BF16)
    y = x + 0.5 * _dot(act, wd_ref[...])
    if final_norm:
        y = _rms_norm(y, fin_ref[...])
    o_ref[...] = y


def _ffn(x, mix, mix_w, g, wg, wu, wd, fin, *, final_norm, tm):
    t, d = x.shape
    row = pl.BlockSpec((tm, d), lambda i: (i, 0))
    mix_specs = [pl.BlockSpec((tm, a.shape[1]), lambda i: (i, 0)) for a in mix]
    return pl.pallas_call(
        functools.partial(_ffn_kernel, n_mix=len(mix), final_norm=final_norm),
        grid=(t // tm,),
        in_specs=[row] + mix_specs + [_resident()] * (len(mix_w) + 5),
        out_specs=row,
        out_shape=jax.ShapeDtypeStruct((t, d), F32),
        compiler_params=_params(1),
        name="ffn_final" if final_norm else ("ffn_mix" if mix else "ffn"),
    )(x, *mix, *mix_w, g, wg, wu, wd, fin)


def _rope_tables(seq_len):
    half = ROT_DIM // 2
    inv = 1.0 / (ROPE_THETA ** (jnp.arange(0, ROT_DIM, 2, dtype=F32) / ROT_DIM))
    ang = jnp.arange(seq_len, dtype=F32)[:, None] * inv[None, :]
    cos, sin = jnp.cos(ang), jnp.sin(ang)
    ones = jnp.ones((seq_len, HEAD_DIM - ROT_DIM), F32)
    zeros_h = jnp.zeros((seq_len, half), F32)
    zeros_r = jnp.zeros((seq_len, HEAD_DIM - ROT_DIM), F32)
    c_same = jnp.concatenate([cos, cos, ones], axis=-1)
    c_prev = jnp.concatenate([zeros_h, sin, zeros_r], axis=-1)
    c_next = jnp.concatenate([-sin, zeros_h, zeros_r], axis=-1)
    rep = LANES // HEAD_DIM
    return tuple(jnp.tile(c, (1, rep)) for c in (c_same, c_prev, c_next))


def _rope(zc, c_same, c_prev, c_next):
    half = ROT_DIM // 2
    return (zc * c_same + pltpu.roll(zc, half, 1) * c_prev
            + pltpu.roll(zc, LANES - half, 1) * c_next)


def _group_mean(a, avg):
    slab = avg.shape[0]
    return jnp.concatenate([_dot(a[:, lo:lo + slab].astype(BF16), avg) for lo in range(0, a.shape[1], slab)], axis=1)


def _even_in_kernel(x_ref, g_ref, w_ref, cs_ref, cp_ref, cn_ref, avg_ref, lng_ref, lnb_ref, ws_ref,
                    bias_ref, q_ref, k_ref, v_ref, km_ref, gated_ref, *, width):
    tm = x_ref.shape[0]
    h = _rms_norm(x_ref[...], g_ref[...]).astype(BF16)
    z = _dot(h, w_ref[...])
    c_same, c_prev, c_next = cs_ref[...], cp_ref[...], cn_ref[...]
    n_lane_blocks = width // LANES
    for cb in range(n_lane_blocks):
        lo = cb * LANES
        q_ref[:, lo:lo + LANES] = (_rope(z[:, lo:lo + LANES], c_same, c_prev, c_next) * QK_SCALE).astype(BF16)
        kr = _rope(z[:, width + lo:width + lo + LANES], c_same, c_prev, c_next)
        k_ref[:, lo:lo + LANES] = kr.astype(BF16)
        for blk in range(tm // MOBA_BLOCK):
            rows = kr[blk * MOBA_BLOCK:(blk + 1) * MOBA_BLOCK]
            km_ref[blk, :, lo:lo + LANES] = jnp.mean(rows, axis=0, keepdims=True)
    v_ref[...] = z[:, 2 * width:3 * width].astype(BF16)

    gz = z[:, 3 * width:]
    gz = 0.5 * gz * (1.0 + jnp.tanh(math.sqrt(2.0 / math.pi) * (gz + 0.044715 * (gz * gz * gz))))
    u, vv = gz[:, :width], gz[:, width:]
    avg = avg_ref[...]
    cen = vv - _group_mean(vv, avg)
    var = _group_mean(cen * cen, avg)
    vn = (cen * lax.rsqrt(var + NORM_EPS) * lng_ref[...] + lnb_ref[...]).astype(BF16)

    t_idx = lax.broadcasted_iota(jnp.int32, (GMLP_CHUNK, GMLP_CHUNK), 0)
    s_idx = lax.broadcasted_iota(jnp.int32, (GMLP_CHUNK, GMLP_CHUNK), 1)
    causal = s_idx <= t_idx
    first_group = lax.broadcasted_iota(jnp.int32, (GMLP_CHUNK, LANES), 1) < GMLP_GROUP_DIM
    chunks = range(0, tm, GMLP_CHUNK)
    for pair in range(n_lane_blocks):
        lo = pair * LANES
        w0 = jnp.where(causal, ws_ref[2 * pair], 0.0).astype(BF16)
        w1 = jnp.where(causal, ws_ref[2 * pair + 1], 0.0).astype(BF16)
        bias = bias_ref[:, lo:lo + LANES]
        vc = jnp.concatenate([vn[r0:r0 + GMLP_CHUNK, lo:lo + LANES] for r0 in chunks], axis=1)
        m0, m1 = _dot(w0, vc), _dot(w1, vc)
        for c, r0 in enumerate(chunks):
            mixed = jnp.where(first_group, m0[:, c * LANES:(c + 1) * LANES], m1[:, c * LANES:(c + 1) * LANES]) + bias
            gated_ref[r0:r0 + GMLP_CHUNK, lo:lo + LANES] = (u[r0:r0 + GMLP_CHUNK, lo:lo + LANES] * mixed).astype(BF16)


def _even_in(x, g, w_in, tables, avg, ln_g, ln_b, w_s, bias, *, seq_len, tm):
    t, d = x.shape
    width = w_in.shape[1] // 5
    n_seq_tiles = seq_len // tm
    row = pl.BlockSpec((tm, d), lambda i: (i, 0))
    tab = pl.BlockSpec((tm, LANES), lambda i: (i % n_seq_tiles, 0))
    out_row = pl.BlockSpec((tm, width), lambda i: (i, 0))
    nb = tm // MOBA_BLOCK
    act = jax.ShapeDtypeStruct((t, width), BF16)
    return pl.pallas_call(
        functools.partial(_even_in_kernel, width=width),
        grid=(t // tm,),
        in_specs=[row, _resident(), _resident(), tab, tab, tab, _resident(), _resident(), _resident(),
                  _resident(), _resident()],
        out_specs=[out_row, out_row, out_row, pl.BlockSpec((nb, 1, width), lambda i: (i, 0, 0)), out_row],
        out_shape=[act, act, act, jax.ShapeDtypeStruct((t // MOBA_BLOCK, 1, width), F32), act],
        compiler_params=_params(1),
        name="even_in",
    )(x, g, w_in, *tables, avg, ln_g, ln_b, w_s, bias)


def _stack_halves(q):
    lane = lax.broadcasted_iota(jnp.int32, q.shape, 1)
    zero = jnp.zeros_like(q)
    return jnp.concatenate([jnp.where(lane < HEAD_DIM, q, zero), jnp.where(lane >= HEAD_DIM, q, zero)], axis=0)


def _flash_scratch(rows, tk, *, row_sums):
    scores, probs = pltpu.VMEM((rows, tk), F32), pltpu.VMEM((rows, tk), BF16)
    state = pltpu.VMEM((rows, LANES), F32)
    return [scores, scores, state, state, probs, probs, state, state, state, state] + [state] * row_sums


def _lane_blocks(s):
    return [s[:, c * LANES:(c + 1) * LANES] for c in range(s.shape[1] // LANES)]


def _score_stage(qk_fn, t, ranges, s_ref, c_ref):
    for r0, n in ranges:
        s = qk_fn(t, r0, n)
        s_ref[r0:r0 + n, :] = s
        c_ref[r0:r0 + n, :] = functools.reduce(jnp.maximum, _lane_blocks(s))


def _softmax_stage(ranges, s_ref, c_ref, p_ref, alpha_ref, m_ref, l_ref, *, start):
    for r0, n, on_diagonal in ranges:
        rows = slice(r0, r0 + n)
        s = s_ref[rows, :]
        if on_diagonal:
            visible = lax.broadcasted_iota(jnp.int32, s.shape, 1) <= lax.broadcasted_iota(jnp.int32, s.shape, 0)
            cols = _lane_blocks(jnp.where(visible, s, NEG_INF))
            col_max = functools.reduce(jnp.maximum, cols)
        else:
            cols = _lane_blocks(s)
            col_max = c_ref[rows, :]
        m_new = jnp.max(col_max, axis=-1, keepdims=True)
        if start:
            m_new = jnp.broadcast_to(m_new, col_max.shape)
        else:
            m_prev = m_ref[rows, :]
            m_new = jnp.maximum(m_prev, m_new)
            alpha = jnp.exp2(m_prev - m_new)
            alpha_ref[rows, :] = alpha
        if l_ref is None:
            p_ref[rows, :] = jnp.concatenate([jnp.exp2((c - m_new).astype(BF16)) for c in cols], axis=1)
        else:
            ps = [jnp.exp2(c - m_new) for c in cols]
            l_new = functools.reduce(jnp.add, ps)
            l_ref[rows, :] = l_new if start else alpha * l_ref[rows, :] + l_new
            p_ref[rows, :] = jnp.concatenate([p.astype(BF16) for p in ps], axis=1)
        m_ref[rows, :] = m_new


def _pv_stage(v_fn, t, ranges, p_ref, alpha_ref, acc_ref, *, start):
    for r0, n in ranges:
        rows = slice(r0, r0 + n)
        pv = _dot(p_ref[rows, :], v_fn(t, r0))
        acc_ref[rows, :] = pv if start else alpha_ref[rows, :] * acc_ref[rows, :] + pv


def _flash_pipeline(qi, tq, qk_fn, v_fn, s0, s1, c0, c1, p0, p1, a0, a1, m_ref, acc_ref, l_ref=None):
    s_refs, c_refs, p_refs, a_refs = (s0, s1), (c0, c1), (p0, p1), (a0, a1)
    rows, tk = s0.shape
    tiles_per_q = tq // tk
    assert tiles_per_q % 2 == 0 and rows == 2 * tq
    n_past = qi * tiles_per_q
    whole = ((0, rows),)
    pv_whole = whole if l_ref is not None else ((0, tq), (tq, tq))

    def seen_by(d):
        return tuple((half * tq + d * tk, tq - d * tk) for half in range(2))

    def softmax_ranges(d):
        out = []
        for half in range(2):
            out.append((half * tq + d * tk, tk, True))
            if (d + 1) * tk < tq:
                out.append((half * tq + (d + 1) * tk, tq - (d + 1) * tk, False))
        return tuple(out)

    def tick(t, slot, pv_ranges, sm_ranges, next_ranges, *, age):
        if age > 0:
            _pv_stage(v_fn, t - 1, pv_ranges, p_refs[1 - slot], a_refs[1 - slot], acc_ref, start=age == 1)
        _softmax_stage(sm_ranges, s_refs[slot], c_refs[slot], p_refs[slot], a_refs[slot], m_ref, l_ref,
                       start=age == 0)
        if next_ranges:
            _score_stage(qk_fn, t + 1, next_ranges, s_refs[1 - slot], c_refs[1 - slot])

    past = ((0, rows, False),)

    def past_pair(i, age):
        tick(2 * i, 0, pv_whole, past, whole, age=age)
        tick(2 * i + 1, 1, pv_whole, past, whole, age=min(age + 1, 2))

    def diagonal_ticks(age):
        for d in range(tiles_per_q):
            tick(n_past + d, d % 2, pv_whole if d == 0 else seen_by(d - 1), softmax_ranges(d),
                 seen_by(d + 1) if d + 1 < tiles_per_q else (), age=min(age + d, 2))

    last = tiles_per_q - 1

    def last_pv():
        _pv_stage(v_fn, n_past + last, seen_by(last), p_refs[last % 2], a_refs[last % 2], acc_ref, start=False)

    @pl.when(qi == 0)
    def _():
        _score_stage(qk_fn, 0, whole, s0, c0)
        diagonal_ticks(0)
        last_pv()

    @pl.when(qi > 0)
    def _():
        _score_stage(qk_fn, 0, whole, s0, c0)
        past_pair(0, 0)
        n_left = qi * (tiles_per_q // 2) - 1
        odd = jnp.bitwise_and(n_left, 1)

        @pl.when(odd == 1)
        def _():
            past_pair(1, 2)

        def body(j, carry):
            i = 1 + odd + 2 * j
            past_pair(i, 2)
            past_pair(i + 1, 2)
            return carry

        lax.fori_loop(0, jnp.right_shift(n_left, 1), body, 0)
        diagonal_ticks(2)
        last_pv()

    if l_ref is None:
        return acc_ref[...]
    return acc_ref[...] / jnp.sum(l_ref[...], axis=-1, keepdims=True)


def _block_bias(gate, own_block, n_blocks):
    block_id = lax.broadcasted_iota(jnp.int32, gate.shape, 0)
    g = jnp.where(block_id < own_block, gate, NEG_INF)
    bias = jnp.where(block_id < n_blocks, jnp.where(block_id == own_block, 0.0, MASKED), 0.0)
    for _ in range(MOBA_TOPK):
        best = jnp.max(g, axis=0, keepdims=True)
        cand = jnp.where(g == best, jnp.where(best > NEG_INF, block_id, LANES), LANES)
        pick = block_id == jnp.min(cand, axis=0, keepdims=True)
        bias = jnp.where(pick, 0.0, bias)
        g = jnp.where(pick, NEG_INF, g)
    return bias


def _moba_kernel(q_ref, k_ref, v_ref, km_ref, o_ref, *scratch, n_blocks):
    tq = q_ref.shape[1]
    log2_block = MOBA_BLOCK.bit_length() - 1
    blocks_per_tile = tq // MOBA_BLOCK
    qi = pl.program_id(2)
    qs = _stack_halves(q_ref[0])
    n_pad = -(-n_blocks // 8) * 8
    gate = _dot_nt(km_ref[0, :n_pad, :].astype(BF16), qs)
    query = jnp.bitwise_and(lax.broadcasted_iota(jnp.int32, gate.shape, 1), tq - 1)
    own_block = qi * blocks_per_tile + jnp.right_shift(query, log2_block)
    bias = _block_bias(gate, own_block, n_blocks)
    if n_pad < LANES:
        bias = jnp.concatenate([bias, jnp.zeros((LANES - n_pad, 2 * tq), F32)], axis=0)
    bias = bias.T
    q_aug = jnp.concatenate([qs, bias.astype(BF16)], axis=1)

    tk = scratch[0].shape[1]
    key_lane = lax.broadcasted_iota(jnp.int32, (tk, LANES), 1)
    key_block = jnp.right_shift(lax.broadcasted_iota(jnp.int32, (tk, LANES), 0), log2_block)

    def scores(t, r0, n):
        one_hot = jnp.where(key_lane == t * (tk // MOBA_BLOCK) + key_block, 1.0, 0.0).astype(BF16)
        k = k_ref[0, pl.ds(pl.multiple_of(t * tk, tk), tk), :]
        return _dot_nt(q_aug[r0:r0 + n], jnp.concatenate([k, one_hot], axis=1))

    def values(t, r0):
        v = v_ref[0, pl.ds(pl.multiple_of(t * tk, tk), tk), :]
        own = key_lane < HEAD_DIM if r0 < tq else key_lane >= HEAD_DIM
        ones_lane = HEAD_DIM if r0 < tq else 0
        return jnp.where(own, v, jnp.where(key_lane == ones_lane, 1.0, 0.0).astype(BF16))

    acc = _flash_pipeline(qi, tq, scores, values, *scratch)
    out_lane = lax.broadcasted_iota(jnp.int32, (tq, LANES), 1)
    o0 = acc[:tq] / acc[:tq, HEAD_DIM:HEAD_DIM + 1]
    o1 = acc[tq:] / acc[tq:, 0:1]
    o_ref[0] = jnp.where(out_lane < HEAD_DIM, o0, o1).astype(BF16)


def _moba(q, k, v, km, *, tq, tk):
    b, s, width = q.shape
    n_blocks = s // MOBA_BLOCK
    assert n_blocks <= LANES and tk % MOBA_BLOCK == 0 and tq & (tq - 1) == 0
    qspec = pl.BlockSpec((1, tq, LANES), lambda bi, hp, qi: (bi, qi, hp))
    kvspec = pl.BlockSpec((1, s, LANES), lambda bi, hp, qi: (bi, 0, hp))
    return pl.pallas_call(
        functools.partial(_moba_kernel, n_blocks=n_blocks),
        grid=(b, width // LANES, s // tq),
        in_specs=[qspec, kvspec, kvspec, pl.BlockSpec((1, LANES, LANES), lambda bi, hp, qi: (bi, 0, hp))],
        out_specs=qspec,
        out_shape=jax.ShapeDtypeStruct((b, s, width), BF16),
        scratch_shapes=_flash_scratch(2 * tq, tk, row_sums=False),
        compiler_params=_params(3),
        name="moba_attn",
    )(q, k, v, km)


def _odd_in_kernel(x_ref, g_ref, w_ref, cs_ref, cp_ref, cn_ref, q_ref, k_ref, v_ref, *, qk_width):
    h = _rms_norm(x_ref[...], g_ref[...]).astype(BF16)
    z = _dot(h, w_ref[...])
    c_same, c_prev, c_next = cs_ref[...], cp_ref[...], cn_ref[...]
    for cb in range(qk_width // LANES):
        lo = cb * LANES
        q_ref[:, lo:lo + LANES] = (_rope(z[:, lo:lo + LANES], c_same, c_prev, c_next) * QK_SCALE).astype(BF16)
        k_ref[:, lo:lo + LANES] = _rope(z[:, qk_width + lo:qk_width + lo + LANES], c_same, c_prev, c_next).astype(BF16)
    v_ref[...] = z[:, 2 * qk_width:].astype(BF16)


def _odd_in(x, g, w_in, tables, *, qk_width, seq_len, tm):
    t, d = x.shape
    v_width = w_in.shape[1] - 2 * qk_width
    n_seq_tiles = seq_len // tm
    row = pl.BlockSpec((tm, d), lambda i: (i, 0))
    tab = pl.BlockSpec((tm, LANES), lambda i: (i % n_seq_tiles, 0))
    qk_row = pl.BlockSpec((tm, qk_width), lambda i: (i, 0))
    qk = jax.ShapeDtypeStruct((t, qk_width), BF16)
    return pl.pallas_call(
        functools.partial(_odd_in_kernel, qk_width=qk_width),
        grid=(t // tm,),
        in_specs=[row, _resident(), _resident(), tab, tab, tab],
        out_specs=[qk_row, qk_row, pl.BlockSpec((tm, v_width), lambda i: (i, 0))],
        out_shape=[qk, qk, jax.ShapeDtypeStruct((t, v_width), BF16)],
        compiler_params=_params(1),
        name="odd_in",
    )(x, g, w_in, *tables)


def _diff_kernel(lq1_ref, lk1_ref, lq2_ref, lk2_ref, g_ref, q_ref, k_ref, v_ref, o_ref, *scratch, lambda_init):
    tq = q_ref.shape[1]
    qs = _stack_halves(q_ref[0])

    tk = scratch[0].shape[1]

    def scores(t, r0, n):
        return _dot_nt(qs[r0:r0 + n], k_ref[0, pl.ds(pl.multiple_of(t * tk, tk), tk), :])

    def values(t, r0):
        return v_ref[0, pl.ds(pl.multiple_of(t * tk, tk), tk), :]

    o = _flash_pipeline(pl.program_id(2), tq, scores, values, *scratch)
    lam = (jnp.exp(jnp.sum(lq1_ref[...] * lk1_ref[...], axis=-1, keepdims=True))
           - jnp.exp(jnp.sum(lq2_ref[...] * lk2_ref[...], axis=-1, keepdims=True)) + lambda_init)
    o = o[:tq] - lam * o[tq:]
    o_ref[0] = (_rms_norm(o, g_ref[...]) * (1.0 - lambda_init)).astype(BF16)


def _diff_attn(q, k, v, lq1, lk1, lq2, lk2, subln_g, *, lambda_init, tq, tk):
    b, s, width = q.shape
    qspec = pl.BlockSpec((1, tq, LANES), lambda bi, h, qi: (bi, qi, h))
    kvspec = pl.BlockSpec((1, s, LANES), lambda bi, h, qi: (bi, 0, h))
    return pl.pallas_call(
        functools.partial(_diff_kernel, lambda_init=lambda_init),
        grid=(b, width // LANES, s // tq),
        in_specs=[_resident()] * 5 + [qspec, kvspec, kvspec],
        out_specs=qspec,
        out_shape=jax.ShapeDtypeStruct((b, s, width), BF16),
        scratch_shapes=_flash_scratch(2 * tq, tk, row_sums=True),
        compiler_params=_params(3),
        name="diff_attn",
    )(lq1, lk1, lq2, lk2, subln_g, q, k, v)


def kernel(x, ffn_pre_norm, ffn_pre_w_gate, ffn_pre_w_up, ffn_pre_w_down, mix_norm, ffn_post_norm, ffn_post_w_gate, ffn_post_w_up, ffn_post_w_down, even_w_in, even_w_out, gmlp_ln_g, gmlp_ln_b, gmlp_w_s, gmlp_b_s, odd_w_in, odd_w_out, diff_lambda_q1, diff_lambda_k1, diff_lambda_q2, diff_lambda_k2, diff_subln_g, final_norm):
    b, s, d = x.shape
    depth = ffn_pre_norm.shape[0]
    t = b * s
    tm = 512
    tables = _rope_tables(s)
    xf = x.reshape(t, d)
    fin = final_norm.reshape(1, d)

    def ffn(xf, mix, mix_w, layer, norm, wg, wu, wd, final):
        return _ffn(xf, mix, mix_w, norm[layer].reshape(1, d), _layer_bf16(wg, layer), _layer_bf16(wu, layer),
                    _layer_bf16(wd, layer), fin, final_norm=final, tm=tm)

    for layer in range(depth):
        xf = ffn(xf, [], [], layer, ffn_pre_norm, ffn_pre_w_gate, ffn_pre_w_up, ffn_pre_w_down, False)
        g_mix = mix_norm[layer].reshape(1, d)
        if layer % 2 == 0:
            e = layer // 2
            width = GMLP_GROUPS * GMLP_GROUP_DIM
            n_blocks = s // MOBA_BLOCK
            group = jnp.arange(2 * LANES) // GMLP_GROUP_DIM
            avg = (group[:, None] == group[None, :]).astype(BF16) / GMLP_GROUP_DIM
            bias = jnp.repeat(gmlp_b_s[e].T, GMLP_GROUP_DIM, axis=1)
            q, k, v, km, gated = _even_in(
                xf, g_mix, _layer_bf16(even_w_in, e), tables, avg, gmlp_ln_g[e].reshape(1, width),
                gmlp_ln_b[e].reshape(1, width), gmlp_w_s[e], bias, seq_len=s, tm=tm)
            km = jnp.pad(km.reshape(b, n_blocks, width), ((0, 0), (0, LANES - n_blocks), (0, 0)))
            attn = _moba(q.reshape(b, s, width), k.reshape(b, s, width), v.reshape(b, s, width), km,
                         tq=min(MOBA_Q_TILE, s), tk=ATTN_KV_TILE)
            mix, mix_w = [attn.reshape(t, width), gated], [_layer_bf16(even_w_out, e)]
        else:
            o = layer // 2
            lambda_init = 0.8 - 0.6 * math.exp(-0.3 * layer)
            v_width = odd_w_out.shape[1]
            qk_width = (odd_w_in.shape[2] - v_width) // 2
            q, k, v = _odd_in(xf, g_mix, _layer_bf16(odd_w_in, o), tables, qk_width=qk_width, seq_len=s, tm=tm)
            vec = lambda a: a[o].reshape(1, -1)
            attn = _diff_attn(q.reshape(b, s, qk_width), k.reshape(b, s, qk_width), v.reshape(b, s, v_width),
                              vec(diff_lambda_q1), vec(diff_lambda_k1), vec(diff_lambda_q2), vec(diff_lambda_k2),
                              vec(diff_subln_g), lambda_init=lambda_init, tq=min(DIFF_Q_TILE, s),
                              tk=ATTN_KV_TILE)
            mix, mix_w = [attn.reshape(t, v_width)], [_layer_bf16(odd_w_out, o)]
        xf = ffn(xf, mix, mix_w, layer, ffn_post_norm, ffn_post_w_gate, ffn_post_w_up, ffn_post_w_down,
                 layer == depth - 1)
    return xf.reshape(b, s, d)
```

```python
import functools
import math

import jax
import jax.numpy as jnp
from jax import lax
from jax.experimental import pallas as pl
from jax.experimental.pallas import tpu as pltpu

F32 = jnp.float32
BF16 = jnp.bfloat16

HEAD_DIM = 64
ROT_DIM = HEAD_DIM // 4
ROPE_THETA = 500000.0
NORM_EPS = 1e-6
MOBA_BLOCK = 256
MOBA_TOPK = 3
GMLP_GROUPS = 8
GMLP_GROUP_DIM = 64
GMLP_CHUNK = 128

LANES = 128
VMEM_LIMIT_BYTES = 56 * 1024 * 1024
MOBA_Q_TILE = 2048
DIFF_Q_TILE = 2048
ATTN_KV_TILE = 512

NEG_INF = float("-inf")
MASKED = -1e30
QK_SCALE = HEAD_DIM ** -0.5 * math.log2(math.e)


def _params(n_grid_axes):
    return pltpu.CompilerParams(
        dimension_semantics=("parallel",) * (n_grid_axes - 1) + ("arbitrary",),
        vmem_limit_bytes=VMEM_LIMIT_BYTES,
    )


def _resident():
    return pl.BlockSpec(memory_space=pltpu.VMEM)


def _rms_norm(x, g):
    ms = jnp.mean(x * x, axis=-1, keepdims=True)
    return x * lax.rsqrt(ms + NORM_EPS) * g


def _dot(a, b):
    return jnp.dot(a, b, preferred_element_type=F32)


def _dot_nt(a, b):
    return lax.dot_general(a, b, (((1,), (1,)), ((), ())), preferred_element_type=F32)


def _cast_kernel(x_ref, o_ref):
    o_ref[...] = x_ref[...].astype(o_ref.dtype)


def _layer_bf16(w, layer):
    _, rows, cols = w.shape
    budget = 3 * 1024 * 1024
    rb = max(r for r in range(16, rows + 1, 16) if rows % r == 0 and r * cols * 4 <= budget)
    return pl.pallas_call(
        _cast_kernel,
        grid=(rows // rb,),
        in_specs=[pl.BlockSpec((None, rb, cols), lambda i: (layer, i, 0))],
        out_specs=pl.BlockSpec((rb, cols), lambda i: (i, 0)),
        out_shape=jax.ShapeDtypeStruct((rows, cols), BF16),
        compiler_params=_params(1),
        name="to_bf16",
    )(w)


def _ffn_kernel(*refs, n_mix, final_norm):
    x_ref, mix_refs = refs[0], refs[1:1 + n_mix]
    mix_w_ref = refs[1 + n_mix] if n_mix else None
    g_ref, wg_ref, wu_ref, wd_ref, fin_ref, o_ref = refs[1 + n_mix + (1 if n_mix else 0):]
    x = x_ref[...]
    row = 0
    for a_ref in mix_refs:
        x = x + _dot(a_ref[...], mix_w_ref[row:row + a_ref.shape[1], :])
        row += a_ref.shape[1]
    h = _rms_norm(x, g_ref[...]).astype(BF16)
    gate = _dot(h, wg_ref[...])
    up = _dot(h, wu_ref[...])
    act = (gate * jax.nn.sigmoid(gate) * up).astype(BF16)
    y = x + 0.5 * _dot(act, wd_ref[...])
    if final_norm:
        y = _rms_norm(y, fin_ref[...])
    o_ref[...] = y


def _ffn(x, mix, mix_w, g, wg, wu, wd, fin, *, final_norm, tm):
    t, d = x.shape
    row = pl.BlockSpec((tm, d), lambda i: (i, 0))
    mix_specs = [pl.BlockSpec((tm, a.shape[1]), lambda i: (i, 0)) for a in mix]
    return pl.pallas_call(
        functools.partial(_ffn_kernel, n_mix=len(mix), final_norm=final_norm),
        grid=(t // tm,),
        in_specs=[row] + mix_specs + [_resident()] * (len(mix_w) + 5),
        out_specs=row,
        out_shape=jax.ShapeDtypeStruct((t, d), F32),
        compiler_params=_params(1),
        name="ffn_final" if final_norm else ("ffn_mix" if mix else "ffn"),
    )(x, *mix, *mix_w, g, wg, wu, wd, fin)


def _rope_tables(seq_len):
    half = ROT_DIM // 2
    inv = 1.0 / (ROPE_THETA ** (jnp.arange(0, ROT_DIM, 2, dtype=F32) / ROT_DIM))
    ang = jnp.arange(seq_len, dtype=F32)[:, None] * inv[None, :]
    cos, sin = jnp.cos(ang), jnp.sin(ang)
    ones = jnp.ones((seq_len, HEAD_DIM - ROT_DIM), F32)
    zeros_h = jnp.zeros((seq_len, half), F32)
    zeros_r = jnp.zeros((seq_len, HEAD_DIM - ROT_DIM), F32)
    c_same = jnp.concatenate([cos, cos, ones], axis=-1)
    c_prev = jnp.concatenate([zeros_h, sin, zeros_r], axis=-1)
    c_next = jnp.concatenate([-sin, zeros_h, zeros_r], axis=-1)
    rep = LANES // HEAD_DIM
    return tuple(jnp.tile(c, (1, rep)) for c in (c_same, c_prev, c_next))


def _rope(zc, c_same, c_prev, c_next):
    half = ROT_DIM // 2
    return (zc * c_same + pltpu.roll(zc, half, 1) * c_prev
            + pltpu.roll(zc, LANES - half, 1) * c_next)


def _group_mean(a, avg):
    slab = avg.shape[0]
    return jnp.concatenate([_dot(a[:, lo:lo + slab].astype(BF16), avg) for lo in range(0, a.shape[1], slab)], axis=1)


def _even_in_kernel(x_ref, g_ref, w_ref, cs_ref, cp_ref, cn_ref, avg_ref, lng_ref, lnb_ref, ws_ref,
                    bias_ref, q_ref, k_ref, v_ref, km_ref, gated_ref, *, width):
    tm = x_ref.shape[0]
    h = _rms_norm(x_ref[...], g_ref[...]).astype(BF16)
    z = _dot(h, w_ref[...])
    c_same, c_prev, c_next = cs_ref[...], cp_ref[...], cn_ref[...]
    n_lane_blocks = width // LANES
    for cb in range(n_lane_blocks):
        lo = cb * LANES
        q_ref[:, lo:lo + LANES] = (_rope(z[:, lo:lo + LANES], c_same, c_prev, c_next) * QK_SCALE).astype(BF16)
        kr = _rope(z[:, width + lo:width + lo + LANES], c_same, c_prev, c_next)
        k_ref[:, lo:lo + LANES] = kr.astype(BF16)
        for blk in range(tm // MOBA_BLOCK):
            rows = kr[blk * MOBA_BLOCK:(blk + 1) * MOBA_BLOCK]
            km_ref[blk, :, lo:lo + LANES] = jnp.mean(rows, axis=0, keepdims=True)
    v_ref[...] = z[:, 2 * width:3 * width].astype(BF16)

    gz = z[:, 3 * width:]
    gz = 0.5 * gz * (1.0 + jnp.tanh(math.sqrt(2.0 / math.pi) * (gz + 0.044715 * (gz * gz * gz))))
    u, vv = gz[:, :width], gz[:, width:]
    avg = avg_ref[...]
    cen = vv - _group_mean(vv, avg)
    var = _group_mean(cen * cen, avg)
    vn = (cen * lax.rsqrt(var + NORM_EPS) * lng_ref[...] + lnb_ref[...]).astype(BF16)

    t_idx = lax.broadcasted_iota(jnp.int32, (GMLP_CHUNK, GMLP_CHUNK), 0)
    s_idx = lax.broadcasted_iota(jnp.int32, (GMLP_CHUNK, GMLP_CHUNK), 1)
    causal = s_idx <= t_idx
    first_group = lax.broadcasted_iota(jnp.int32, (GMLP_CHUNK, LANES), 1) < GMLP_GROUP_DIM
    chunks = range(0, tm, GMLP_CHUNK)
    for pair in range(n_lane_blocks):
        lo = pair * LANES
        w0 = jnp.where(causal, ws_ref[2 * pair], 0.0).astype(BF16)
        w1 = jnp.where(causal, ws_ref[2 * pair + 1], 0.0).astype(BF16)
        bias = bias_ref[:, lo:lo + LANES]
        vc = jnp.concatenate([vn[r0:r0 + GMLP_CHUNK, lo:lo + LANES] for r0 in chunks], axis=1)
        m0, m1 = _dot(w0, vc), _dot(w1, vc)
        for c, r0 in enumerate(chunks):
            mixed = jnp.where(first_group, m0[:, c * LANES:(c + 1) * LANES], m1[:, c * LANES:(c + 1) * LANES]) + bias
            gated_ref[r0:r0 + GMLP_CHUNK, lo:lo + LANES] = (u[r0:r0 + GMLP_CHUNK, lo:lo + LANES] * mixed).astype(BF16)


def _even_in(x, g, w_in, tables, avg, ln_g, ln_b, w_s, bias, *, seq_len, tm):
    t, d = x.shape
    width = w_in.shape[1] // 5
    n_seq_tiles = seq_len // tm
    row = pl.BlockSpec((tm, d), lambda i: (i, 0))
    tab = pl.BlockSpec((tm, LANES), lambda i: (i % n_seq_tiles, 0))
    out_row = pl.BlockSpec((tm, width), lambda i: (i, 0))
    nb = tm // MOBA_BLOCK
    act = jax.ShapeDtypeStruct((t, width), BF16)
    return pl.pallas_call(
        functools.partial(_even_in_kernel, width=width),
        grid=(t // tm,),
        in_specs=[row, _resident(), _resident(), tab, tab, tab, _resident(), _resident(), _resident(),
                  _resident(), _resident()],
        out_specs=[out_row, out_row, out_row, pl.BlockSpec((nb, 1, width), lambda i: (i, 0, 0)), out_row],
        out_shape=[act, act, act, jax.ShapeDtypeStruct((t // MOBA_BLOCK, 1, width), F32), act],
        compiler_params=_params(1),
        name="even_in",
    )(x, g, w_in, *tables, avg, ln_g, ln_b, w_s, bias)


def _stack_halves(q):
    lane = lax.broadcasted_iota(jnp.int32, q.shape, 1)
    zero = jnp.zeros_like(q)
    return jnp.concatenate([jnp.where(lane < HEAD_DIM, q, zero), jnp.where(lane >= HEAD_DIM, q, zero)], axis=0)


def _flash_scratch(rows, tk, *, row_sums):
    scores, probs = pltpu.VMEM((rows, tk), F32), pltpu.VMEM((rows, tk), BF16)
    state = pltpu.VMEM((rows, LANES), F32)
    return [scores, scores, state, state, probs, probs, state, state, state, state] + [state] * row_sums


def _lane_blocks(s):
    return [s[:, c * LANES:(c + 1) * LANES] for c in range(s.shape[1] // LANES)]


def _score_stage(qk_fn, t, ranges, s_ref, c_ref):
    for r0, n in ranges:
        s = qk_fn(t, r0, n)
        s_ref[r0:r0 + n, :] = s
        c_ref[r0:r0 + n, :] = functools.reduce(jnp.maximum, _lane_blocks(s))


def _softmax_stage(ranges, s_ref, c_ref, p_ref, alpha_ref, m_ref, l_ref, *, start):
    for r0, n, on_diagonal in ranges:
        rows = slice(r0, r0 + n)
        s = s_ref[rows, :]
        if on_diagonal:
            visible = lax.broadcasted_iota(jnp.int32, s.shape, 1) <= lax.broadcasted_iota(jnp.int32, s.shape, 0)
            cols = _lane_blocks(jnp.where(visible, s, NEG_INF))
            col_max = functools.reduce(jnp.maximum, cols)
        else:
            cols = _lane_blocks(s)
            col_max = c_ref[rows, :]
        m_new = jnp.max(col_max, axis=-1, keepdims=True)
        if start:
            m_new = jnp.broadcast_to(m_new, col_max.shape)
        else:
            m_prev = m_ref[rows, :]
            m_new = jnp.maximum(m_prev, m_new)
            alpha = jnp.exp2(m_prev - m_new)
            alpha_ref[rows, :] = alpha
        if l_ref is None:
            p_ref[rows, :] = jnp.concatenate([jnp.exp2((c - m_new).astype(BF16)) for c in cols], axis=1)
        else:
            ps = [jnp.exp2(c - m_new) for c in cols]
            l_new = functools.reduce(jnp.add, ps)
            l_ref[rows, :] = l_new if start else alpha * l_ref[rows, :] + l_new
            p_ref[rows, :] = jnp.concatenate([p.astype(BF16) for p in ps], axis=1)
        m_ref[rows, :] = m_new


def _pv_stage(v_fn, t, ranges, p_ref, alpha_ref, acc_ref, *, start):
    for r0, n in ranges:
        rows = slice(r0, r0 + n)
        pv = _dot(p_ref[rows, :], v_fn(t, r0))
        acc_ref[rows, :] = pv if start else alpha_ref[rows, :] * acc_ref[rows, :] + pv


def _flash_pipeline(qi, tq, qk_fn, v_fn, s0, s1, c0, c1, p0, p1, a0, a1, m_ref, acc_ref, l_ref=None):
    s_refs, c_refs, p_refs, a_refs = (s0, s1), (c0, c1), (p0, p1), (a0, a1)
    rows, tk = s0.shape
    tiles_per_q = tq // tk
    assert tiles_per_q % 2 == 0 and rows == 2 * tq
    n_past = qi * tiles_per_q
    whole = ((0, rows),)
    pv_whole = whole if l_ref is not None else ((0, tq), (tq, tq))

    def seen_by(d):
        return tuple((half * tq + d * tk, tq - d * tk) for half in range(2))

    def softmax_ranges(d):
        out = []
        for half in range(2):
            out.append((half * tq + d * tk, tk, True))
            if (d + 1) * tk < tq:
                out.append((half * tq + (d + 1) * tk, tq - (d + 1) * tk, False))
        return tuple(out)

    def tick(t, slot, pv_ranges, sm_ranges, next_ranges, *, age):
        if age > 0:
            _pv_stage(v_fn, t - 1, pv_ranges, p_refs[1 - slot], a_refs[1 - slot], acc_ref, start=age == 1)
        _softmax_stage(sm_ranges, s_refs[slot], c_refs[slot], p_refs[slot], a_refs[slot], m_ref, l_ref,
                       start=age == 0)
        if next_ranges:
            _score_stage(qk_fn, t + 1, next_ranges, s_refs[1 - slot], c_refs[1 - slot])

    past = ((0, rows, False),)

    def past_pair(i, age):
        tick(2 * i, 0, pv_whole, past, whole, age=age)
        tick(2 * i + 1, 1, pv_whole, past, whole, age=min(age + 1, 2))

    def diagonal_ticks(age):
        for d in range(tiles_per_q):
            tick(n_past + d, d % 2, pv_whole if d == 0 else seen_by(d - 1), softmax_ranges(d),
                 seen_by(d + 1) if d + 1 < tiles_per_q else (), age=min(age + d, 2))

    last = tiles_per_q - 1

    def last_pv():
        _pv_stage(v_fn, n_past + last, seen_by(last), p_refs[last % 2], a_refs[last % 2], acc_ref, start=False)

    @pl.when(qi == 0)
    def _():
        _score_stage(qk_fn, 0, whole, s0, c0)
        diagonal_ticks(0)
        last_pv()

    @pl.when(qi > 0)
    def _():
        _score_stage(qk_fn, 0, whole, s0, c0)
        past_pair(0, 0)
        n_left = qi * (tiles_per_q // 2) - 1
        odd = jnp.bitwise_and(n_left, 1)

        @pl.when(odd == 1)
        def _():
            past_pair(1, 2)

        def body(j, carry):
            i = 1 + odd + 2 * j
            past_pair(i, 2)
            past_pair(i + 1, 2)
            return carry

        lax.fori_loop(0, jnp.right_shift(n_left, 1), body, 0)
        diagonal_ticks(2)
        last_pv()

    if l_ref is None:
        return acc_ref[...]
    return acc_ref[...] / jnp.sum(l_ref[...], axis=-1, keepdims=True)


def _block_bias(gate, own_block, n_blocks):
    block_id = lax.broadcasted_iota(jnp.int32, gate.shape, 0)
    g = jnp.where(block_id < own_block, gate, NEG_INF)
    bias = jnp.where(block_id < n_blocks, jnp.where(block_id == own_block, 0.0, MASKED), 0.0)
    for _ in range(MOBA_TOPK):
        best = jnp.max(g, axis=0, keepdims=True)
        cand = jnp.where(g == best, jnp.where(best > NEG_INF, block_id, LANES), LANES)
        pick = block_id == jnp.min(cand, axis=0, keepdims=True)
        bias = jnp.where(pick, 0.0, bias)
        g = jnp.where(pick, NEG_INF, g)
    return bias


def _moba_kernel(q_ref, k_ref, v_ref, km_ref, o_ref, *scratch, n_blocks):
    tq = q_ref.shape[1]
    log2_block = MOBA_BLOCK.bit_length() - 1
    blocks_per_tile = tq // MOBA_BLOCK
    qi = pl.program_id(2)
    qs = _stack_halves(q_ref[0])
    n_pad = -(-n_blocks // 8) * 8
    gate = _dot_nt(km_ref[0, :n_pad, :].astype(BF16), qs)
    query = jnp.bitwise_and(lax.broadcasted_iota(jnp.int32, gate.shape, 1), tq - 1)
    own_block = qi * blocks_per_tile + jnp.right_shift(query, log2_block)
    bias = _block_bias(gate, own_block, n_blocks)
    if n_pad < LANES:
        bias = jnp.concatenate([bias, jnp.zeros((LANES - n_pad, 2 * tq), F32)], axis=0)
    bias = bias.T
    q_aug = jnp.concatenate([qs, bias.astype(BF16)], axis=1)

    tk = scratch[0].shape[1]
    key_lane = lax.broadcasted_iota(jnp.int32, (tk, LANES), 1)
    key_block = jnp.right_shift(lax.broadcasted_iota(jnp.int32, (tk, LANES), 0), log2_block)

    def scores(t, r0, n):
        one_hot = jnp.where(key_lane == t * (tk // MOBA_BLOCK) + key_block, 1.0, 0.0).astype(BF16)
        k = k_ref[0, pl.ds(pl.multiple_of(t * tk, tk), tk), :]
        return _dot_nt(q_aug[r0:r0 + n], jnp.concatenate([k, one_hot], axis=1))

    def values(t, r0):
        v = v_ref[0, pl.ds(pl.multiple_of(t * tk, tk), tk), :]
        own = key_lane < HEAD_DIM if r0 < tq else key_lane >= HEAD_DIM
        ones_lane = HEAD_DIM if r0 < tq else 0
        return jnp.where(own, v, jnp.where(key_lane == ones_lane, 1.0, 0.0).astype(BF16))

    acc = _flash_pipeline(qi, tq, scores, values, *scratch)
    out_lane = lax.broadcasted_iota(jnp.int32, (tq, LANES), 1)
    o0 = acc[:tq] / acc[:tq, HEAD_DIM:HEAD_DIM + 1]
    o1 = acc[tq:] / acc[tq:, 0:1]
    o_ref[0] = jnp.where(out_lane < HEAD_DIM, o0, o1).astype(BF16)


def _moba(q, k, v, km, *, tq, tk):
    b, s, width = q.shape
    n_blocks = s // MOBA_BLOCK
    assert n_blocks <= LANES and tk % MOBA_BLOCK == 0 and tq & (tq - 1) == 0
    qspec = pl.BlockSpec((1, tq, LANES), lambda bi, hp, qi: (bi, qi, hp))
    kvspec = pl.BlockSpec((1, s, LANES), lambda bi, hp, qi: (bi, 0, hp))
    return pl.pallas_call(
        functools.partial(_moba_kernel, n_blocks=n_blocks),
        grid=(b, width // LANES, s // tq),
        in_specs=[qspec, kvspec, kvspec, pl.BlockSpec((1, LANES, LANES), lambda bi, hp, qi: (bi, 0, hp))],
        out_specs=qspec,
        out_shape=jax.ShapeDtypeStruct((b, s, width), BF16),
        scratch_shapes=_flash_scratch(2 * tq, tk, row_sums=False),
        compiler_params=_params(3),
        name="moba_attn",
    )(q, k, v, km)


def _odd_in_kernel(x_ref, g_ref, w_ref, cs_ref, cp_ref, cn_ref, q_ref, k_ref, v_ref, *, qk_width):
    h = _rms_norm(x_ref[...], g_ref[...]).astype(BF16)
    z = _dot(h, w_ref[...])
    c_same, c_prev, c_next = cs_ref[...], cp_ref[...], cn_ref[...]
    for cb in range(qk_width // LANES):
        lo = cb * LANES
        q_ref[:, lo:lo + LANES] = (_rope(z[:, lo:lo + LANES], c_same, c_prev, c_next) * QK_SCALE).astype(BF16)
        k_ref[:, lo:lo + LANES] = _rope(z[:, qk_width + lo:qk_width + lo + LANES], c_same, c_prev, c_next).astype(BF16)
    v_ref[...] = z[:, 2 * qk_width:].astype(BF16)


def _odd_in(x, g, w_in, tables, *, qk_width, seq_len, tm):
    t, d = x.shape
    v_width = w_in.shape[1] - 2 * qk_width
    n_seq_tiles = seq_len // tm
    row = pl.BlockSpec((tm, d), lambda i: (i, 0))
    tab = pl.BlockSpec((tm, LANES), lambda i: (i % n_seq_tiles, 0))
    qk_row = pl.BlockSpec((tm, qk_width), lambda i: (i, 0))
    qk = jax.ShapeDtypeStruct((t, qk_width), BF16)
    return pl.pallas_call(
        functools.partial(_odd_in_kernel, qk_width=qk_width),
        grid=(t // tm,),
        in_specs=[row, _resident(), _resident(), tab, tab, tab],
        out_specs=[qk_row, qk_row, pl.BlockSpec((tm, v_width), lambda i: (i, 0))],
        out_shape=[qk, qk, jax.ShapeDtypeStruct((t, v_width), BF16)],
        compiler_params=_params(1),
        name="odd_in",
    )(x, g, w_in, *tables)


def _diff_kernel(lq1_ref, lk1_ref, lq2_ref, lk2_ref, g_ref, q_ref, k_ref, v_ref, o_ref, *scratch, lambda_init):
    tq = q_ref.shape[1]
    qs = _stack_halves(q_ref[0])

    tk = scratch[0].shape[1]

    def scores(t, r0, n):
        return _dot_nt(qs[r0:r0 + n], k_ref[0, pl.ds(pl.multiple_of(t * tk, tk), tk), :])

    def values(t, r0):
        return v_ref[0, pl.ds(pl.multiple_of(t * tk, tk), tk), :]

    o = _flash_pipeline(pl.program_id(2), tq, scores, values, *scratch)
    lam = (jnp.exp(jnp.sum(lq1_ref[...] * lk1_ref[...], axis=-1, keepdims=True))
           - jnp.exp(jnp.sum(lq2_ref[...] * lk2_ref[...], axis=-1, keepdims=True)) + lambda_init)
    o = o[:tq] - lam * o[tq:]
    o_ref[0] = (_rms_norm(o, g_ref[...]) * (1.0 - lambda_init)).astype(BF16)


def _diff_attn(q, k, v, lq1, lk1, lq2, lk2, subln_g, *, lambda_init, tq, tk):
    b, s, width = q.shape
    qspec = pl.BlockSpec((1, tq, LANES), lambda bi, h, qi: (bi, qi, h))
    kvspec = pl.BlockSpec((1, s, LANES), lambda bi, h, qi: (bi, 0, h))
    return pl.pallas_call(
        functools.partial(_diff_kernel, lambda_init=lambda_init),
        grid=(b, width // LANES, s // tq),
        in_specs=[_resident()] * 5 + [qspec, kvspec, kvspec],
        out_specs=qspec,
        out_shape=jax.ShapeDtypeStruct((b, s, width), BF16),
        scratch_shapes=_flash_scratch(2 * tq, tk, row_sums=True),
        compiler_params=_params(3),
        name="diff_attn",
    )(lq1, lk1, lq2, lk2, subln_g, q, k, v)


def kernel(x, ffn_pre_norm, ffn_pre_w_gate, ffn_pre_w_up, ffn_pre_w_down, mix_norm, ffn_post_norm, ffn_post_w_gate, ffn_post_w_up, ffn_post_w_down, even_w_in, even_w_out, gmlp_ln_g, gmlp_ln_b, gmlp_w_s, gmlp_b_s, odd_w_in, odd_w_out, diff_lambda_q1, diff_lambda_k1, diff_lambda_q2, diff_lambda_k2, diff_subln_g, final_norm):
    b, s, d = x.shape
    depth = ffn_pre_norm.shape[0]
    t = b * s
    tm = 512
    tables = _rope_tables(s)
    xf = x.reshape(t, d)
    fin = final_norm.reshape(1, d)

    def ffn(xf, mix, mix_w, layer, norm, wg, wu, wd, final):
        return _ffn(xf, mix, mix_w, norm[layer].reshape(1, d), _layer_bf16(wg, layer), _layer_bf16(wu, layer),
                    _layer_bf16(wd, layer), fin, final_norm=final, tm=tm)

    for layer in range(depth):
        xf = ffn(xf, [], [], layer, ffn_pre_norm, ffn_pre_w_gate, ffn_pre_w_up, ffn_pre_w_down, False)
        g_mix = mix_norm[layer].reshape(1, d)
        if layer % 2 == 0:
            e = layer // 2
            width = GMLP_GROUPS * GMLP_GROUP_DIM
            n_blocks = s // MOBA_BLOCK
            group = jnp.arange(2 * LANES) // GMLP_GROUP_DIM
            avg = (group[:, None] == group[None, :]).astype(BF16) / GMLP_GROUP_DIM
            bias = jnp.repeat(gmlp_b_s[e].T, GMLP_GROUP_DIM, axis=1)
            q, k, v, km, gated = _even_in(
                xf, g_mix, _layer_bf16(even_w_in, e), tables, avg, gmlp_ln_g[e].reshape(1, width),
                gmlp_ln_b[e].reshape(1, width), gmlp_w_s[e], bias, seq_len=s, tm=tm)
            km = jnp.pad(km.reshape(b, n_blocks, width), ((0, 0), (0, LANES - n_blocks), (0, 0)))
            attn = _moba(q.reshape(b, s, width), k.reshape(b, s, width), v.reshape(b, s, width), km,
                         tq=min(MOBA_Q_TILE, s), tk=ATTN_KV_TILE)
            mix, mix_w = [attn.reshape(t, width), gated], [_layer_bf16(even_w_out, e)]
        else:
            o = layer // 2
            lambda_init = 0.8 - 0.6 * math.exp(-0.3 * layer)
            v_width = odd_w_out.shape[1]
            qk_width = (odd_w_in.shape[2] - v_width) // 2
            q, k, v = _odd_in(xf, g_mix, _layer_bf16(odd_w_in, o), tables, qk_width=qk_width, seq_len=s, tm=tm)
            vec = lambda a: a[o].reshape(1, -1)
            attn = _diff_attn(q.reshape(b, s, qk_width), k.reshape(b, s, qk_width), v.reshape(b, s, v_width),
                              vec(diff_lambda_q1), vec(diff_lambda_k1), vec(diff_lambda_q2), vec(diff_lambda_k2),
                              vec(diff_subln_g), lambda_init=lambda_init, tq=min(DIFF_Q_TILE, s),
                              tk=ATTN_KV_TILE)
            mix, mix_w = [attn.reshape(t, v_width)], [_layer_bf16(odd_w_out, o)]
        xf = ffn(xf, mix, mix_w, layer, ffn_post_norm, ffn_post_w_gate, ffn_post_w_up, ffn_post_w_down,
                 layer == depth - 1)
    return xf.reshape(b, s, d)
```
